```python
import math
import jax
import jax.numpy as jnp
from jax import lax
import numpy as np

D_MODEL = 1024
BATCH = 16
SEQ = 2048
DEPTH = 4

MEM_LEN = 256
HG_HEADS = 4
HG_DK = 128
HG_DV = 128
HG_WIDTH = HG_HEADS * HG_DV
GD_HEADS = 4
GD_DK = 128
GD_DV = 128
GD_WIDTH = GD_HEADS * GD_DV
MIX_WIDTH = HG_WIDTH + GD_WIDTH
CONV_K = 4
GD_CONV_WIDTH = 2 * GD_HEADS * GD_DK + GD_WIDTH
CHUNK = 64
IN_SPLITS = (HG_HEADS * HG_DK, HG_HEADS * HG_DK, HG_WIDTH, HG_WIDTH,
             GD_HEADS * GD_DK, GD_HEADS * GD_DK, GD_WIDTH, GD_HEADS, GD_HEADS, GD_WIDTH)
IN_WIDTH = sum(IN_SPLITS)
MEM_HEADS = 4
MEM_DH = D_MODEL // MEM_HEADS
N_GROUPS = 4
EXPERTS_PER_GROUP = 8
N_EXPERTS = N_GROUPS * EXPERTS_PER_GROUP
TOP_K = 2
D_EXPERT = 512
EXPERT_BLOCK = 128
ALPHA = (2.0 * DEPTH) ** 0.25
BETA = (8.0 * DEPTH) ** -0.25
LN_EPS = 1e-5
RMS_EPS = 1e-6
L2_EPS = 1e-6

kernel_name = 'hybrid_hgrn2_gdn_hmoe_trunk'


def layer_norm(x, g, b):
    xf = x.astype(jnp.float32)
    mu = jnp.mean(xf, axis=-1, keepdims=True)
    var = jnp.mean(jnp.square(xf - mu), axis=-1, keepdims=True)
    return ((xf - mu) * lax.rsqrt(var + LN_EPS) * g + b).astype(x.dtype)


def gated_rms_norm(o, w, gate):
    o = o * lax.rsqrt(jnp.mean(o * o, axis=-1, keepdims=True) + RMS_EPS) * w.astype(jnp.float32)
    return o * gate


def l2_normalize(x):
    return x * lax.rsqrt(jnp.sum(x * x, axis=-1, keepdims=True) + L2_EPS)


def causal_depthwise_conv_silu(x, w):
    y = lax.conv_general_dilated(
        x, w[:, None, :], window_strides=(1,), padding=[(CONV_K - 1, 0)],
        dimension_numbers=('NWC', 'WIO', 'NWC'), feature_group_count=x.shape[-1])
    return jax.nn.silu(y)


def to_chunks(x):
    b, t, h, d = x.shape
    return x.reshape(b, t // CHUNK, CHUNK, h, d).transpose(1, 0, 3, 2, 4)


def from_chunks(x):
    n, b, h, c, d = x.shape
    return x.transpose(1, 0, 3, 2, 4).reshape(b, n * c, h, d)


def hgrn2_chunked(q, k, v, log_f):
    qc, kc, vc, gc = to_chunks(q), to_chunks(k), to_chunks(v), to_chunks(log_f)
    causal = jnp.tril(jnp.ones((CHUNK, CHUNK), dtype=bool))
    bsz, h = q.shape[0], q.shape[2]

    def step(S, inp):
        qi, ki, vi, gi = inp
        cum = jnp.cumsum(gi, axis=2)
        diff = cum[:, :, :, None, :] - cum[:, :, None, :, :]
        decay = jnp.exp(jnp.where(causal[:, :, None], diff, -jnp.inf))
        scores = jnp.einsum('bhtd,bhsd,bhtsd->bhts', qi, ki, decay)
        o = (jnp.einsum('bhtd,bhdv->bhtv', qi * jnp.exp(cum), S)
             + jnp.einsum('bhts,bhsv->bhtv', scores, vi))
        cum_last = cum[:, :, -1, :]
        S = (jnp.exp(cum_last)[..., None] * S
             + jnp.einsum('bhsd,bhsv->bhdv', ki * jnp.exp(cum_last[:, :, None, :] - cum), vi))
        return S, o

    S0 = jnp.zeros((bsz, h, q.shape[-1], v.shape[-1]), jnp.float32)
    _, o = lax.scan(step, S0, (qc, kc, vc, gc))
    return from_chunks(o)


def gated_delta_chunked(q, k, v, g, beta):
    qc, kc, vc = to_chunks(q), to_chunks(k), to_chunks(v)
    gc = to_chunks(g[..., None])[..., 0]
    bc = to_chunks(beta[..., None])[..., 0]
    cum = jnp.cumsum(gc, axis=-1)
    diff = cum[..., :, None] - cum[..., None, :]
    incl = jnp.tril(jnp.ones((CHUNK, CHUNK), dtype=bool))
    strict = jnp.tril(jnp.ones((CHUNK, CHUNK), dtype=bool), -1)
    decay = jnp.exp(jnp.where(incl, diff, -jnp.inf))
    kb = kc * bc[..., None]
    a_mat = jnp.where(strict, jnp.einsum('nbhtd,nbhsd->nbhts', kb, kc) * decay, 0.0)
    u = lax.linalg.triangular_solve(a_mat, vc * bc[..., None], left_side=True, lower=True,
                                    unit_diagonal=True)
    w = lax.linalg.triangular_solve(a_mat, kb * jnp.exp(cum)[..., None], left_side=True,
                                    lower=True, unit_diagonal=True)
    qk = jnp.einsum('nbhtd,nbhsd->nbhts', qc, kc) * decay
    bsz, h = q.shape[0], q.shape[2]

    def step(S, inp):
        qi, ki, ui, wi, ci, qki = inp
        v_new = ui - jnp.einsum('bhtd,bhdv->bhtv', wi, S)
        o = (jnp.einsum('bhtd,bhdv->bhtv', qi * jnp.exp(ci)[..., None], S)
             + jnp.einsum('bhts,bhsv->bhtv', qki, v_new))
        c_last = ci[..., -1:]
        S = (jnp.exp(c_last)[..., None] * S
             + jnp.einsum('bhsd,bhsv->bhdv', ki * jnp.exp(c_last - ci)[..., None], v_new))
        return S, o

    S0 = jnp.zeros((bsz, h, q.shape[-1], v.shape[-1]), jnp.float32)
    _, o = lax.scan(step, S0, (qc, kc, u, w, cum, qk))
    return from_chunks(o)


def hybrid_mixer(x, w_in, w_out, hg_lb, hg_norm_w, gd_conv_w, gd_a_log, gd_dt_bias, gd_norm_w):
    b, t, _ = x.shape
    proj = (x @ w_in).astype(jnp.float32)
    offs = np.cumsum(IN_SPLITS)[:-1].tolist()
    hq, hf, hi, hz, gq, gk, gv, ga, gb, gz = jnp.split(proj, offs, axis=-1)

    lb = hg_lb.astype(jnp.float32).reshape(HG_HEADS, HG_DK)
    zf = hf.reshape(b, t, HG_HEADS, HG_DK)
    log_f = jnp.logaddexp(jnp.log(lb), jnp.log1p(-lb) + jax.nn.log_sigmoid(zf))
    k_hg = (1.0 - lb) * jax.nn.sigmoid(-zf)
    o_hg = hgrn2_chunked(hq.reshape(b, t, HG_HEADS, HG_DK), k_hg,
                         hi.reshape(b, t, HG_HEADS, HG_DV), log_f)
    y_hg = gated_rms_norm(o_hg, hg_norm_w, jax.nn.sigmoid(hz.reshape(b, t, HG_HEADS, HG_DV)))

    qkv = causal_depthwise_conv_silu(jnp.concatenate([gq, gk, gv], axis=-1),
                                     gd_conv_w.astype(jnp.float32))
    cq, ck, cv = jnp.split(qkv, [GD_HEADS * GD_DK, 2 * GD_HEADS * GD_DK], axis=-1)
    q_gd = l2_normalize(cq.reshape(b, t, GD_HEADS, GD_DK)) * (GD_DK ** -0.5)
    k_gd = l2_normalize(ck.reshape(b, t, GD_HEADS, GD_DK))
    v_gd = cv.reshape(b, t, GD_HEADS, GD_DV)
    g_gd = -jnp.exp(gd_a_log.astype(jnp.float32)) * jax.nn.softplus(ga + gd_dt_bias.astype(jnp.float32))
    beta = jax.nn.sigmoid(gb)
    o_gd = gated_delta_chunked(q_gd, k_gd, v_gd, g_gd, beta)
    y_gd = gated_rms_norm(o_gd, gd_norm_w, jax.nn.silu(gz.reshape(b, t, GD_HEADS, GD_DV)))

    y = jnp.concatenate([y_hg.reshape(b, t, HG_WIDTH), y_gd.reshape(b, t, GD_WIDTH)], axis=-1)
    return y.astype(x.dtype) @ w_out


def memory_cross_attention(x, mem_n, wq, wk, wv, wo):
    b, t, _ = x.shape
    m = mem_n.shape[1]
    q = (x @ wq).reshape(b, t, MEM_HEADS, MEM_DH)
    k = (mem_n @ wk).reshape(b, m, MEM_HEADS, MEM_DH)
    v = (mem_n @ wv).reshape(b, m, MEM_HEADS, MEM_DH)
    s = jnp.einsum('bthd,bmhd->bhtm', q, k).astype(jnp.float32) * (MEM_DH ** -0.5)
    p = jax.nn.softmax(s, axis=-1).astype(v.dtype)
    o = jnp.einsum('bhtm,bmhd->bthd', p, v).reshape(b, t, D_MODEL)
    return o @ wo


def hierarchical_moe(x, w_group, b_group, w_router, b_router, w_gate, w_up, w_down):
    b, t, d = x.shape
    n_tok = b * t
    xt = x.reshape(n_tok, d)
    tok = jnp.arange(n_tok)
    group_logits = (xt @ w_group).astype(jnp.float32) + b_group.astype(jnp.float32)
    group_p = jax.nn.softmax(group_logits, axis=-1)
    g_sel = jnp.argmax(group_logits, axis=-1)
    g_gate = group_p[tok, g_sel]
    exp_logits = ((xt @ w_router).astype(jnp.float32) + b_router.astype(jnp.float32)
                  ).reshape(n_tok, N_GROUPS, EXPERTS_PER_GROUP)
    local_logits = exp_logits[tok, g_sel]
    top_logits, top_local = lax.top_k(local_logits, TOP_K)
    gate = jax.nn.softmax(top_logits, axis=-1) * g_gate[:, None]
    expert_id = g_sel[:, None] * EXPERTS_PER_GROUP + top_local

    m = n_tok * TOP_K
    flat_e = expert_id.reshape(m)
    flat_tok = jnp.arange(m) // TOP_K
    flat_w = gate.reshape(m)
    order = jnp.argsort(flat_e)
    se, stok, sw = flat_e[order], flat_tok[order], flat_w[order]
    counts = jnp.zeros((N_EXPERTS,), jnp.int32).at[flat_e].add(1)
    starts = jnp.cumsum(counts) - counts
    pcounts = (counts + EXPERT_BLOCK - 1) // EXPERT_BLOCK * EXPERT_BLOCK
    pends = jnp.cumsum(pcounts)
    pstarts = pends - pcounts
    dest = pstarts[se] + jnp.arange(m) - starts[se]
    n_blocks = -(-m // EXPERT_BLOCK) + N_EXPERTS
    buf = jnp.zeros((n_blocks * EXPERT_BLOCK, d), x.dtype).at[dest].set(xt[stok])
    block_expert = jnp.minimum(
        jnp.searchsorted(pends, jnp.arange(n_blocks) * EXPERT_BLOCK, side='right'), N_EXPERTS - 1)

    def expert_block(args):
        xb, e = args
        h = jax.nn.silu(xb @ w_gate[e]) * (xb @ w_up[e])
        return h @ w_down[e]

    ybuf = lax.map(expert_block, (buf.reshape(n_blocks, EXPERT_BLOCK, d), block_expert))
    y = ybuf.reshape(n_blocks * EXPERT_BLOCK, d)[dest] * sw[:, None].astype(x.dtype)
    out = jnp.zeros((n_tok, d), x.dtype).at[stok].add(y)
    return out.reshape(b, t, d)


def setup_inputs(seed: int = 0) -> dict:
    key = jax.random.key(seed)
    ks = jax.random.split(key, 28)
    f32 = jnp.float32
    D = D_MODEL

    def nrm(k, shape, scale):
        return jax.random.normal(k, shape, f32) * scale

    x = nrm(ks[0], (BATCH, SEQ, D), 1.0)
    mem = nrm(ks[1], (BATCH, MEM_LEN, D), 1.0)
    w_in = nrm(ks[2], (DEPTH, D, IN_WIDTH), D ** -0.5)
    hg_lb_logits = nrm(ks[3], (DEPTH, HG_HEADS * HG_DK), 0.5)
    hg_norm_w = 1.0 + nrm(ks[4], (DEPTH, HG_DV), 0.02)
    gd_conv_w = nrm(ks[5], (DEPTH, CONV_K, GD_CONV_WIDTH), CONV_K ** -0.5)
    gd_a_log = jnp.log(jax.random.uniform(ks[6], (DEPTH, GD_HEADS), f32, 1.0, 16.0))
    dt = jnp.exp(jax.random.uniform(ks[7], (DEPTH, GD_HEADS), f32, math.log(1e-3), math.log(1e-1)))
    gd_dt_bias = dt + jnp.log(-jnp.expm1(-dt))
    gd_norm_w = 1.0 + nrm(ks[8], (DEPTH, GD_DV), 0.02)
    w_out = nrm(ks[9], (DEPTH, MIX_WIDTH, D), MIX_WIDTH ** -0.5 * BETA)
    mem_ln_g = 1.0 + nrm(ks[10], (D,), 0.02)
    mem_ln_b = nrm(ks[11], (D,), 0.02)
    w_mq = nrm(ks[12], (DEPTH, D, D), D ** -0.5)
    w_mk = nrm(ks[13], (DEPTH, D, D), D ** -0.5)
    w_mv = nrm(ks[14], (DEPTH, D, D), D ** -0.5)
    w_mo = nrm(ks[15], (DEPTH, D, D), D ** -0.5 * BETA)
    w_group = nrm(ks[16], (DEPTH, D, N_GROUPS), D ** -0.5)
    b_group = nrm(ks[17], (DEPTH, N_GROUPS), 0.01)
    w_router = nrm(ks[18], (DEPTH, D, N_EXPERTS), D ** -0.5)
    b_router = nrm(ks[19], (DEPTH, N_EXPERTS), 0.01)
    w_gate = nrm(ks[20], (DEPTH, N_EXPERTS, D, D_EXPERT), D ** -0.5)
    w_up = nrm(ks[21], (DEPTH, N_EXPERTS, D, D_EXPERT), D ** -0.5)
    w_down = nrm(ks[22], (DEPTH, N_EXPERTS, D_EXPERT, D), D_EXPERT ** -0.5 * BETA)
    ln_g = 1.0 + nrm(ks[23], (DEPTH, 3, D), 0.02)
    ln_b = nrm(ks[24], (DEPTH, 3, D), 0.02)
    return {'x': x, 'mem': mem, 'w_in': w_in, 'hg_lb_logits': hg_lb_logits,
            'hg_norm_w': hg_norm_w, 'gd_conv_w': gd_conv_w, 'gd_a_log': gd_a_log,
            'gd_dt_bias': gd_dt_bias, 'gd_norm_w': gd_norm_w, 'w_out': w_out,
            'mem_ln_g': mem_ln_g, 'mem_ln_b': mem_ln_b, 'w_mq': w_mq, 'w_mk': w_mk,
            'w_mv': w_mv, 'w_mo': w_mo, 'w_group': w_group, 'b_group': b_group,
            'w_router': w_router, 'b_router': b_router, 'w_gate': w_gate, 'w_up': w_up,
            'w_down': w_down, 'ln_g': ln_g, 'ln_b': ln_b}


def reference(x, mem, w_in, hg_lb_logits, hg_norm_w, gd_conv_w, gd_a_log, gd_dt_bias,
              gd_norm_w, w_out, mem_ln_g, mem_ln_b, w_mq, w_mk, w_mv, w_mo, w_group, b_group,
              w_router, b_router, w_gate, w_up, w_down, ln_g, ln_b):
    p_lb = jax.nn.softmax(hg_lb_logits.astype(jnp.float32), axis=0)
    lb_all = jnp.cumsum(p_lb, axis=0)
    lb_all = lb_all - lb_all[0]
    mem_n = layer_norm(mem, mem_ln_g, mem_ln_b)
    for l in range(DEPTH):
        mix = hybrid_mixer(x, w_in[l], w_out[l], lb_all[l], hg_norm_w[l], gd_conv_w[l],
                           gd_a_log[l], gd_dt_bias[l], gd_norm_w[l])
        x = layer_norm(ALPHA * x + mix, ln_g[l, 0], ln_b[l, 0])
        xa = memory_cross_attention(x, mem_n, w_mq[l], w_mk[l], w_mv[l], w_mo[l])
        x = layer_norm(ALPHA * x + xa, ln_g[l, 1], ln_b[l, 1])
        ff = hierarchical_moe(x, w_group[l], b_group[l], w_router[l], b_router[l],
                              w_gate[l], w_up[l], w_down[l])
        x = layer_norm(ALPHA * x + ff, ln_g[l, 2], ln_b[l, 2])
    return x
```

```python
import math
import jax
import jax.numpy as jnp
from jax import lax
import numpy as np
from jax.experimental import pallas as pl

D_MODEL = 1024
BATCH = 16
SEQ = 2048
DEPTH = 4
MEM_LEN = 256
HG_HEADS = 4
HG_DK = 128
HG_DV = 128
HG_WIDTH = HG_HEADS * HG_DV
GD_HEADS = 4
GD_DK = 128
GD_DV = 128
GD_WIDTH = GD_HEADS * GD_DV
MIX_WIDTH = HG_WIDTH + GD_WIDTH
CONV_K = 4
GD_CONV_WIDTH = 2 * GD_HEADS * GD_DK + GD_WIDTH
CHUNK = 64
IN_SPLITS = (HG_HEADS * HG_DK, HG_HEADS * HG_DK, HG_WIDTH, HG_WIDTH,
             GD_HEADS * GD_DK, GD_HEADS * GD_DK, GD_WIDTH, GD_HEADS, GD_HEADS, GD_WIDTH)
IN_WIDTH = sum(IN_SPLITS)
MEM_HEADS = 4
MEM_DH = D_MODEL // MEM_HEADS
N_GROUPS = 4
EXPERTS_PER_GROUP = 8
N_EXPERTS = N_GROUPS * EXPERTS_PER_GROUP
TOP_K = 2
D_EXPERT = 512
EXPERT_BLOCK = 128
ALPHA = (2.0 * DEPTH) ** 0.25
BETA = (8.0 * DEPTH) ** -0.25
LN_EPS = 1e-5
RMS_EPS = 1e-6
L2_EPS = 1e-6


def layer_norm(x, g, b):
    xf = x.astype(jnp.float32)
    mu = jnp.mean(xf, axis=-1, keepdims=True)
    var = jnp.mean(jnp.square(xf - mu), axis=-1, keepdims=True)
    return ((xf - mu) * lax.rsqrt(var + LN_EPS) * g + b).astype(x.dtype)


def gated_rms_norm(o, w, gate):
    o = o * lax.rsqrt(jnp.mean(o * o, axis=-1, keepdims=True) + RMS_EPS) * w.astype(jnp.float32)
    return o * gate


def l2_normalize(x):
    return x * lax.rsqrt(jnp.sum(x * x, axis=-1, keepdims=True) + L2_EPS)


def causal_depthwise_conv_silu(x, w):
    y = lax.conv_general_dilated(
        x, w[:, None, :], window_strides=(1,), padding=[(CONV_K - 1, 0)],
        dimension_numbers=('NWC', 'WIO', 'NWC'), feature_group_count=x.shape[-1])
    return jax.nn.silu(y)


def to_chunks(x):
    b, t, h, d = x.shape
    return x.reshape(b, t // CHUNK, CHUNK, h, d).transpose(1, 0, 3, 2, 4)


def from_chunks(x):
    n, b, h, c, d = x.shape
    return x.transpose(1, 0, 3, 2, 4).reshape(b, n * c, h, d)


def hgrn2_chunked(q, k, v, log_f):
    qc, kc, vc, gc = to_chunks(q), to_chunks(k), to_chunks(v), to_chunks(log_f)
    causal = jnp.tril(jnp.ones((CHUNK, CHUNK), dtype=bool))
    bsz, h = q.shape[0], q.shape[2]

    def step(S, inp):
        qi, ki, vi, gi = inp
        cum = jnp.cumsum(gi, axis=2)
        diff = cum[:, :, :, None, :] - cum[:, :, None, :, :]
        decay = jnp.exp(jnp.where(causal[:, :, None], diff, -jnp.inf))
        scores = jnp.einsum('bhtd,bhsd,bhtsd->bhts', qi, ki, decay)
        o = (jnp.einsum('bhtd,bhdv->bhtv', qi * jnp.exp(cum), S)
             + jnp.einsum('bhts,bhsv->bhtv', scores, vi))
        cum_last = cum[:, :, -1, :]
        S = (jnp.exp(cum_last)[..., None] * S
             + jnp.einsum('bhsd,bhsv->bhdv', ki * jnp.exp(cum_last[:, :, None, :] - cum), vi))
        return S, o

    S0 = jnp.zeros((bsz, h, q.shape[-1], v.shape[-1]), jnp.float32)
    _, o = lax.scan(step, S0, (qc, kc, vc, gc))
    return from_chunks(o)


def gated_delta_chunked(q, k, v, g, beta):
    qc, kc, vc = to_chunks(q), to_chunks(k), to_chunks(v)
    gc = to_chunks(g[..., None])[..., 0]
    bc = to_chunks(beta[..., None])[..., 0]
    cum = jnp.cumsum(gc, axis=-1)
    diff = cum[..., :, None] - cum[..., None, :]
    incl = jnp.tril(jnp.ones((CHUNK, CHUNK), dtype=bool))
    strict = jnp.tril(jnp.ones((CHUNK, CHUNK), dtype=bool), -1)
    decay = jnp.exp(jnp.where(incl, diff, -jnp.inf))
    kb = kc * bc[..., None]
    a_mat = jnp.where(strict, jnp.einsum('nbhtd,nbhsd->nbhts', kb, kc) * decay, 0.0)
    u = lax.linalg.triangular_solve(a_mat, vc * bc[..., None], left_side=True, lower=True,
                                    unit_diagonal=True)
    w = lax.linalg.triangular_solve(a_mat, kb * jnp.exp(cum)[..., None], left_side=True,
                                    lower=True, unit_diagonal=True)
    qk = jnp.einsum('nbhtd,nbhsd->nbhts', qc, kc) * decay
    bsz, h = q.shape[0], q.shape[2]

    def step(S, inp):
        qi, ki, ui, wi, ci, qki = inp
        v_new = ui - jnp.einsum('bhtd,bhdv->bhtv', wi, S)
        o = (jnp.einsum('bhtd,bhdv->bhtv', qi * jnp.exp(ci)[..., None], S)
             + jnp.einsum('bhts,bhsv->bhtv', qki, v_new))
        c_last = ci[..., -1:]
        S = (jnp.exp(c_last)[..., None] * S
             + jnp.einsum('bhsd,bhsv->bhdv', ki * jnp.exp(c_last - ci)[..., None], v_new))
        return S, o

    S0 = jnp.zeros((bsz, h, q.shape[-1], v.shape[-1]), jnp.float32)
    _, o = lax.scan(step, S0, (qc, kc, u, w, cum, qk))
    return from_chunks(o)


def hybrid_mixer(x, w_in, w_out, hg_lb, hg_norm_w, gd_conv_w, gd_a_log, gd_dt_bias, gd_norm_w):
    b, t, _ = x.shape
    proj = (x @ w_in).astype(jnp.float32)
    offs = np.cumsum(IN_SPLITS)[:-1].tolist()
    hq, hf, hi, hz, gq, gk, gv, ga, gb, gz = jnp.split(proj, offs, axis=-1)
    lb = hg_lb.astype(jnp.float32).reshape(HG_HEADS, HG_DK)
    zf = hf.reshape(b, t, HG_HEADS, HG_DK)
    log_f = jnp.logaddexp(jnp.log(lb), jnp.log1p(-lb) + jax.nn.log_sigmoid(zf))
    k_hg = (1.0 - lb) * jax.nn.sigmoid(-zf)
    o_hg = hgrn2_chunked(hq.reshape(b, t, HG_HEADS, HG_DK), k_hg,
                         hi.reshape(b, t, HG_HEADS, HG_DV), log_f)
    y_hg = gated_rms_norm(o_hg, hg_norm_w, jax.nn.sigmoid(hz.reshape(b, t, HG_HEADS, HG_DV)))
    qkv = causal_depthwise_conv_silu(jnp.concatenate([gq, gk, gv], axis=-1),
                                     gd_conv_w.astype(jnp.float32))
    cq, ck, cv = jnp.split(qkv, [GD_HEADS * GD_DK, 2 * GD_HEADS * GD_DK], axis=-1)
    q_gd = l2_normalize(cq.reshape(b, t, GD_HEADS, GD_DK)) * (GD_DK ** -0.5)
    k_gd = l2_normalize(ck.reshape(b, t, GD_HEADS, GD_DK))
    v_gd = cv.reshape(b, t, GD_HEADS, GD_DV)
    g_gd = -jnp.exp(gd_a_log.astype(jnp.float32)) * jax.nn.softplus(ga + gd_dt_bias.astype(jnp.float32))
    beta = jax.nn.sigmoid(gb)
    o_gd = gated_delta_chunked(q_gd, k_gd, v_gd, g_gd, beta)
    y_gd = gated_rms_norm(o_gd, gd_norm_w, jax.nn.silu(gz.reshape(b, t, GD_HEADS, GD_DV)))
    y = jnp.concatenate([y_hg.reshape(b, t, HG_WIDTH), y_gd.reshape(b, t, GD_WIDTH)], axis=-1)
    return y.astype(x.dtype) @ w_out


def memory_cross_attention(x, mem_n, wq, wk, wv, wo):
    b, t, _ = x.shape
    m = mem_n.shape[1]
    q = (x @ wq).reshape(b, t, MEM_HEADS, MEM_DH)
    k = (mem_n @ wk).reshape(b, m, MEM_HEADS, MEM_DH)
    v = (mem_n @ wv).reshape(b, m, MEM_HEADS, MEM_DH)
    s = jnp.einsum('bthd,bmhd->bhtm', q, k).astype(jnp.float32) * (MEM_DH ** -0.5)
    p = jax.nn.softmax(s, axis=-1).astype(v.dtype)
    o = jnp.einsum('bhtm,bmhd->bthd', p, v).reshape(b, t, D_MODEL)
    return o @ wo


def hierarchical_moe(x, w_group, b_group, w_router, b_router, w_gate, w_up, w_down):
    b, t, d = x.shape
    n_tok = b * t
    xt = x.reshape(n_tok, d)
    tok = jnp.arange(n_tok)
    group_logits = (xt @ w_group).astype(jnp.float32) + b_group.astype(jnp.float32)
    group_p = jax.nn.softmax(group_logits, axis=-1)
    g_sel = jnp.argmax(group_logits, axis=-1)
    g_gate = group_p[tok, g_sel]
    exp_logits = ((xt @ w_router).astype(jnp.float32) + b_router.astype(jnp.float32)
                  ).reshape(n_tok, N_GROUPS, EXPERTS_PER_GROUP)
    local_logits = exp_logits[tok, g_sel]
    top_logits, top_local = lax.top_k(local_logits, TOP_K)
    gate = jax.nn.softmax(top_logits, axis=-1) * g_gate[:, None]
    expert_id = g_sel[:, None] * EXPERTS_PER_GROUP + top_local
    m = n_tok * TOP_K
    flat_e = expert_id.reshape(m)
    flat_tok = jnp.arange(m) // TOP_K
    flat_w = gate.reshape(m)
    order = jnp.argsort(flat_e)
    se, stok, sw = flat_e[order], flat_tok[order], flat_w[order]
    counts = jnp.zeros((N_EXPERTS,), jnp.int32).at[flat_e].add(1)
    starts = jnp.cumsum(counts) - counts
    pcounts = (counts + EXPERT_BLOCK - 1) // EXPERT_BLOCK * EXPERT_BLOCK
    pends = jnp.cumsum(pcounts)
    pstarts = pends - pcounts
    dest = pstarts[se] + jnp.arange(m) - starts[se]
    n_blocks = -(-m // EXPERT_BLOCK) + N_EXPERTS
    buf = jnp.zeros((n_blocks * EXPERT_BLOCK, d), x.dtype).at[dest].set(xt[stok])
    block_expert = jnp.minimum(
        jnp.searchsorted(pends, jnp.arange(n_blocks) * EXPERT_BLOCK, side='right'), N_EXPERTS - 1)

    def expert_block(args):
        xb, e = args
        h = jax.nn.silu(xb @ w_gate[e]) * (xb @ w_up[e])
        return h @ w_down[e]

    ybuf = lax.map(expert_block, (buf.reshape(n_blocks, EXPERT_BLOCK, d), block_expert))
    y = ybuf.reshape(n_blocks * EXPERT_BLOCK, d)[dest] * sw[:, None].astype(x.dtype)
    out = jnp.zeros((n_tok, d), x.dtype).at[stok].add(y)
    return out.reshape(b, t, d)


def _add_ln_kernel(x_ref, y_ref, g_ref, b_ref, o_ref):
    z = ALPHA * x_ref[...] + y_ref[...]
    mu = jnp.mean(z, axis=-1, keepdims=True)
    zc = z - mu
    var = jnp.mean(zc * zc, axis=-1, keepdims=True)
    o_ref[...] = zc * lax.rsqrt(var + LN_EPS) * g_ref[...] + b_ref[...]


def pallas_add_ln(x, y, g, b):
    bsz, t, d = x.shape
    n = bsz * t
    tm = 512
    out = pl.pallas_call(
        _add_ln_kernel,
        grid=(n // tm,),
        in_specs=[pl.BlockSpec((tm, d), lambda i: (i, 0)),
                  pl.BlockSpec((tm, d), lambda i: (i, 0)),
                  pl.BlockSpec((1, d), lambda i: (0, 0)),
                  pl.BlockSpec((1, d), lambda i: (0, 0))],
        out_specs=pl.BlockSpec((tm, d), lambda i: (i, 0)),
        out_shape=jax.ShapeDtypeStruct((n, d), x.dtype),
        name="add_ln",
    )(x.reshape(n, d), y.reshape(n, d), g.reshape(1, d), b.reshape(1, d))
    return out.reshape(bsz, t, d)


def kernel(x, mem, w_in, hg_lb_logits, hg_norm_w, gd_conv_w, gd_a_log, gd_dt_bias,
           gd_norm_w, w_out, mem_ln_g, mem_ln_b, w_mq, w_mk, w_mv, w_mo, w_group, b_group,
           w_router, b_router, w_gate, w_up, w_down, ln_g, ln_b):
    p_lb = jax.nn.softmax(hg_lb_logits.astype(jnp.float32), axis=0)
    lb_all = jnp.cumsum(p_lb, axis=0)
    lb_all = lb_all - lb_all[0]
    mem_n = layer_norm(mem, mem_ln_g, mem_ln_b)
    for l in range(DEPTH):
        mix = hybrid_mixer(x, w_in[l], w_out[l], lb_all[l], hg_norm_w[l], gd_conv_w[l],
                           gd_a_log[l], gd_dt_bias[l], gd_norm_w[l])
        x = layer_norm(ALPHA * x + mix, ln_g[l, 0], ln_b[l, 0])
        xa = memory_cross_attention(x, mem_n, w_mq[l], w_mk[l], w_mv[l], w_mo[l])
        x = layer_norm(ALPHA * x + xa, ln_g[l, 1], ln_b[l, 1])
        ff = hierarchical_moe(x, w_group[l], b_group[l], w_router[l], b_router[l],
                              w_gate[l], w_up[l], w_down[l])
        x = pallas_add_ln(x, ff, ln_g[l, 2], ln_b[l, 2])
    return x
```

```python
import functools
import math

import jax
import jax.numpy as jnp
import numpy as np
from jax import lax
from jax.experimental import pallas as pl
from jax.experimental.pallas import tpu as pltpu

D_MODEL = 1024
DEPTH = 4
HG_HEADS = 4
HG_DK = 128
GD_HEADS = 4
GD_DK = 128
HEAD_W = 128
CONV_K = 4
MEM_HEADS = 4
MEM_DH = D_MODEL // MEM_HEADS
N_GROUPS = 4
EXPERTS_PER_GROUP = 8
N_EXPERTS = N_GROUPS * EXPERTS_PER_GROUP
TOP_K = 2
D_EXPERT = 512
ALPHA = (2.0 * DEPTH) ** 0.25
LN_EPS = 1e-5
RMS_EPS = 1e-6
L2_EPS = 1e-6

COL_HQ, COL_HF, COL_HI, COL_HZ, COL_GQ, COL_GK, COL_GV, COL_GZ, COL_GAB = 0, 4, 8, 12, 16, 20, 24, 28, 32
PROJ_W = 33 * HEAD_W

TIME_BLOCK = 256
GD_CHUNK = 64
ROW_TILE = 512
EXPERT_ROWS = 256
GATHER_TILE = 256
ROUTE_W = 128
VMEM_LIMIT_BYTES = 48 * 1024 * 1024

BF16 = jnp.bfloat16
F32 = jnp.float32
NT_DIMS = (((1,), (1,)), ((), ()))
TN_DIMS = (((0,), (0,)), ((), ()))


def _params(n_grid):
    return pltpu.CompilerParams(dimension_semantics=("arbitrary",) * n_grid,
                                vmem_limit_bytes=VMEM_LIMIT_BYTES)


def _mm(a, b):
    return jnp.dot(a.astype(BF16), b.astype(BF16), preferred_element_type=F32)


def _mm_nt(a, b):
    return lax.dot_general(a.astype(BF16), b.astype(BF16), NT_DIMS, preferred_element_type=F32)


def _mm_tn(a, b):
    return lax.dot_general(a.astype(BF16), b.astype(BF16), TN_DIMS, preferred_element_type=F32)


def _cumsum_rows(tri, x):
    p0 = x.astype(BF16)
    r0 = x - p0.astype(F32)
    p1 = r0.astype(BF16)
    p2 = (r0 - p1.astype(F32)).astype(BF16)
    dot = lambda p: jnp.dot(tri, p, preferred_element_type=F32)
    return dot(p0) + dot(p1) + dot(p2)


def _sigmoid(x):
    e = jnp.exp(-jnp.abs(x))
    return jnp.where(x >= 0, 1.0, e) / (1.0 + e)


def _softplus(x):
    return jnp.maximum(x, 0.0) + jnp.log1p(jnp.exp(-jnp.abs(x)))


def _layer_norm(z, g, b):
    mu = jnp.mean(z, axis=-1, keepdims=True)
    zc = z - mu
    var = jnp.mean(zc * zc, axis=-1, keepdims=True)
    return zc * lax.rsqrt(var + LN_EPS) * g + b


def _matmul_kernel(x_ref, w_ref, o_ref):
    o_ref[...] = _mm(x_ref[...], w_ref[...]).astype(o_ref.dtype)


def _matmul(x, w, out_dtype, tm, tn):
    m, k = x.shape
    n = w.shape[1]
    return pl.pallas_call(
        _matmul_kernel,
        grid=(m // tm, n // tn),
        in_specs=[pl.BlockSpec((tm, k), lambda i, j: (i, 0)),
                  pl.BlockSpec((k, tn), lambda i, j: (0, j))],
        out_specs=pl.BlockSpec((tm, tn), lambda i, j: (i, j)),
        out_shape=jax.ShapeDtypeStruct((m, n), out_dtype),
        compiler_params=_params(2),
        name="matmul",
    )(x, w)


def _ln_kernel(x_ref, g_ref, b_ref, o_ref):
    o_ref[...] = _layer_norm(x_ref[...], g_ref[...], b_ref[...])


def _ln(x, g, b, tm):
    n, d = x.shape
    return pl.pallas_call(
        _ln_kernel,
        grid=(n // tm,),
        in_specs=[pl.BlockSpec((tm, d), lambda i: (i, 0)),
                  pl.BlockSpec((1, d), lambda i: (0, 0)),
                  pl.BlockSpec((1, d), lambda i: (0, 0))],
        out_specs=pl.BlockSpec((tm, d), lambda i: (i, 0)),
        out_shape=jax.ShapeDtypeStruct((n, d), F32),
        compiler_params=_params(1),
        name="mem_ln",
    )(x, g.reshape(1, d), b.reshape(1, d))


def _ref_rows(cum, row, h):
    c, w = cum.shape
    if 2 * h <= 8:
        j = row & (2 * h - 1)
        m = cum
        for dl in range(-(h - 1), h + 1):
            if dl != 0:
                m = jnp.where(j - (h - 1) == dl, pltpu.roll(cum, dl % c, axis=0), m)
        return m
    pieces = [jnp.broadcast_to(cum[s + h - 1:s + h, :], (2 * h, w)) for s in range(0, c, 2 * h)]
    return jnp.concatenate(pieces, axis=0) if len(pieces) > 1 else pieces[0]


def _hgrn_kernel(q_ref, f_ref, i_ref, z_ref, lb_ref, nw_ref, tri_ref, o_ref, st_ref):
    @pl.when(pl.program_id(2) == 0)
    def _():
        st_ref[...] = jnp.zeros_like(st_ref)

    c = q_ref.shape[0]
    q, zf, v = q_ref[...], f_ref[...], i_ref[...]
    log_lb, log1m_lb, one_m_lb = lb_ref[0:1, :], lb_ref[1:2, :], lb_ref[2:3, :]
    e = jnp.exp(-jnp.abs(zf))
    log_sig = jnp.minimum(zf, 0.0) - jnp.log1p(e)
    b = log1m_lb + log_sig
    log_f = jnp.maximum(log_lb, b) + jnp.log1p(jnp.exp(-jnp.abs(log_lb - b)))
    k = one_m_lb * (jnp.where(zf >= 0, e, 1.0) / (1.0 + e))
    cum = _cumsum_rows(tri_ref[...], log_f)

    row = lax.broadcasted_iota(jnp.int32, (c, 1), 0)
    col = lax.broadcasted_iota(jnp.int32, (1, c), 1)
    scores = jnp.where(row == col, jnp.sum(q * k, axis=-1, keepdims=True), 0.0)
    h = 1
    while h < c:
        x = jnp.exp(-jnp.abs(cum - _ref_rows(cum, row, h)))
        upper = (row & (2 * h - 1)) >= h
        sc = _mm_nt(jnp.where(upper, q * x, 0.0), jnp.where(upper, 0.0, k * x))
        if 2 * h < c:
            shift = int(math.log2(2 * h))
            sc = jnp.where((row >> shift) == (col >> shift), sc, 0.0)
        scores = scores + sc
        h *= 2

    st = st_ref[...]
    o = _mm_nt(q * jnp.exp(cum), st) + _mm(scores, v)
    cum_last = cum[c - 1:c, :]
    st_ref[...] = jnp.exp(cum_last) * st + _mm_tn(v, k * jnp.exp(cum_last - cum))
    o = o * lax.rsqrt(jnp.mean(o * o, axis=-1, keepdims=True) + RMS_EPS) * nw_ref[...]
    o_ref[...] = (o * _sigmoid(z_ref[...])).astype(o_ref.dtype)


def _hgrn(proj, lb_tab, norm_w, bsz, seq):
    nt = seq // TIME_BLOCK
    tri = jnp.asarray(np.tril(np.ones((TIME_BLOCK, TIME_BLOCK), np.float32)), BF16)
    col = lambda c0: pl.BlockSpec((TIME_BLOCK, HEAD_W), lambda b, h, t: (b * nt + t, c0 + h))
    return pl.pallas_call(
        _hgrn_kernel,
        grid=(bsz, HG_HEADS, nt),
        in_specs=[col(COL_HQ), col(COL_HF), col(COL_HI), col(COL_HZ),
                  pl.BlockSpec((3, HEAD_W), lambda b, h, t: (0, h)),
                  pl.BlockSpec((1, HEAD_W), lambda b, h, t: (0, 0)),
                  pl.BlockSpec((TIME_BLOCK, TIME_BLOCK), lambda b, h, t: (0, 0))],
        out_specs=pl.BlockSpec((TIME_BLOCK, HEAD_W), lambda b, h, t: (b * nt + t, h)),
        out_shape=jax.ShapeDtypeStruct((bsz * seq, HG_HEADS * HEAD_W), BF16),
        scratch_shapes=[pltpu.VMEM((HEAD_W, HG_DK), F32)],
        compiler_params=_params(3),
        name="hgrn2",
    )(proj, proj, proj, proj, lb_tab, norm_w.reshape(1, HEAD_W), tri)


def _conv_silu(x_ref, w_ref, buf_ref, first):
    tb = x_ref.shape[0]

    @pl.when(first)
    def _():
        buf_ref[0:8, :] = jnp.zeros((8, buf_ref.shape[1]), F32)

    @pl.when(jnp.logical_not(first))
    def _():
        buf_ref[0:8, :] = buf_ref[tb:tb + 8, :]

    buf_ref[8:tb + 8, :] = x_ref[...]
    y = w_ref[CONV_K - 1:CONV_K, :] * x_ref[...]
    for j in range(1, CONV_K):
        y = y + w_ref[CONV_K - 1 - j:CONV_K - j, :] * buf_ref[8 - j:8 - j + tb, :]
    return y * _sigmoid(y)


def _gdn_kernel(alog_ref, dtb_ref, q_ref, k_ref, v_ref, z_ref, ab_ref, wq_ref, wk_ref, wv_ref,
                nw_ref, tri_ref, o_ref, s_ref, bq_ref, bk_ref, bv_ref):
    head = pl.program_id(1)
    first = pl.program_id(2) == 0

    @pl.when(first)
    def _():
        s_ref[...] = jnp.zeros_like(s_ref)

    tb = q_ref.shape[0]
    cq = _conv_silu(q_ref, wq_ref, bq_ref, first)
    ck = _conv_silu(k_ref, wk_ref, bk_ref, first)
    v = _conv_silu(v_ref, wv_ref, bv_ref, first)
    q = cq * lax.rsqrt(jnp.sum(cq * cq, axis=-1, keepdims=True) + L2_EPS) * (GD_DK ** -0.5)
    k = ck * lax.rsqrt(jnp.sum(ck * ck, axis=-1, keepdims=True) + L2_EPS)

    lane = lax.broadcasted_iota(jnp.int32, (1, HEAD_W), 1)
    ab = ab_ref[...]
    ga = jnp.sum(jnp.where(lane == head, ab, 0.0), axis=-1, keepdims=True)
    gb = jnp.sum(jnp.where(lane == GD_HEADS + head, ab, 0.0), axis=-1, keepdims=True)
    neg_a = -jnp.exp(jnp.full((1, 1), alog_ref[head], F32))
    g = neg_a * _softplus(ga + dtb_ref[head])
    beta = _sigmoid(gb)
    lc = _cumsum_rows(tri_ref[...], jnp.broadcast_to(g, (tb, HEAD_W)))
    lc_col = lc[:, 0:1]
    lc_row = jnp.transpose(lc)[0:1, :]

    row = lax.broadcasted_iota(jnp.int32, (tb, 1), 0)
    col = lax.broadcasted_iota(jnp.int32, (1, tb), 1)
    shift = int(math.log2(GD_CHUNK))
    same = (row >> shift) == (col >> shift)
    decay = jnp.exp(jnp.where(same & (col <= row), lc_col - lc_row, -jnp.inf))
    kb = k * beta
    p = _mm_nt(jnp.concatenate([q, kb], axis=0), k)
    qk = p[:tb] * decay
    a = jnp.where(same & (col < row), p[tb:] * decay, 0.0)
    t = jnp.where(row == col, 1.0, 0.0) - a
    pw = a
    for _ in range(shift - 1):
        pw = _mm(pw, pw)
        t = t + _mm(t, pw)
    e_lc = jnp.exp(lc_col)
    uw = _mm(t, jnp.concatenate([v * beta, kb * e_lc], axis=1))
    u, w = uw[:, :HEAD_W], uw[:, HEAD_W:]
    qe = q * e_lc

    s = s_ref[...]
    outs = []
    for c0 in range(0, tb, GD_CHUNK):
        c1 = c0 + GD_CHUNK
        r = _mm(jnp.concatenate([w[c0:c1], qe[c0:c1]], axis=0), s)
        v_new = u[c0:c1] - r[:GD_CHUNK]
        pads = ([jnp.zeros((c0, HEAD_W), F32)] if c0 else []) + [v_new] + (
            [jnp.zeros((tb - c1, HEAD_W), F32)] if c1 < tb else [])
        outs.append(r[GD_CHUNK:] + _mm(qk[c0:c1, :], jnp.concatenate(pads, axis=0)))
        lc_last = lc[c1 - 1:c1, :]
        kd = k[c0:c1] * jnp.exp(lc_last[:, 0:1] - lc_col[c0:c1])
        s = jnp.exp(lc_last) * s + _mm_tn(kd, v_new)
    s_ref[...] = s
    o = jnp.concatenate(outs, axis=0)
    o = o * lax.rsqrt(jnp.mean(o * o, axis=-1, keepdims=True) + RMS_EPS) * nw_ref[...]
    zg = z_ref[...]
    o_ref[...] = (o * (zg * _sigmoid(zg))).astype(o_ref.dtype)


def _gdn(proj, conv_w, a_log, dt_bias, norm_w, bsz, seq):
    nt = seq // TIME_BLOCK
    idx = np.arange(TIME_BLOCK)
    tri = (idx[:, None] >= idx[None, :]) & (idx[:, None] // GD_CHUNK == idx[None, :] // GD_CHUNK)
    tri = jnp.asarray(tri.astype(np.float32), BF16)
    col = lambda c0: pl.BlockSpec((TIME_BLOCK, HEAD_W), lambda b, h, t: (b * nt + t, c0 + h))
    cw = lambda c0: pl.BlockSpec((CONV_K, HEAD_W), lambda b, h, t: (0, c0 + h))
    smem = pl.BlockSpec(memory_space=pltpu.SMEM)
    return pl.pallas_call(
        _gdn_kernel,
        grid=(bsz, GD_HEADS, nt),
        in_specs=[smem, smem, col(COL_GQ), col(COL_GK), col(COL_GV), col(COL_GZ),
                  pl.BlockSpec((TIME_BLOCK, HEAD_W), lambda b, h, t: (b * nt + t, COL_GAB)),
                  cw(0), cw(GD_HEADS), cw(2 * GD_HEADS),
                  pl.BlockSpec((1, HEAD_W), lambda b, h, t: (0, 0)),
                  pl.BlockSpec((TIME_BLOCK, TIME_BLOCK), lambda b, h, t: (0, 0))],
        out_specs=pl.BlockSpec((TIME_BLOCK, HEAD_W), lambda b, h, t: (b * nt + t, h)),
        out_shape=jax.ShapeDtypeStruct((bsz * seq, GD_HEADS * HEAD_W), BF16),
        scratch_shapes=[pltpu.VMEM((GD_DK, HEAD_W), F32)] + [
            pltpu.VMEM((TIME_BLOCK + 8, HEAD_W), F32) for _ in range(3)],
        compiler_params=_params(3),
        name="gated_deltanet",
    )(a_log, dt_bias, proj, proj, proj, proj, proj, conv_w, conv_w, conv_w,
      norm_w.reshape(1, HEAD_W), tri)


def _outproj_ln_kernel(yh_ref, yg_ref, wh_ref, wg_ref, x_ref, g_ref, b_ref, o_ref):
    mix = (jnp.dot(yh_ref[...], wh_ref[...], preferred_element_type=F32)
           + jnp.dot(yg_ref[...], wg_ref[...], preferred_element_type=F32))
    o_ref[...] = _layer_norm(ALPHA * x_ref[...] + mix, g_ref[...], b_ref[...])


def _outproj_ln(y_hg, y_gd, w_out, layer, x, ln_g, ln_b):
    n, d = x.shape
    hw = y_hg.shape[1]
    tm = ROW_TILE
    return pl.pallas_call(
        _outproj_ln_kernel,
        grid=(n // tm,),
        in_specs=[pl.BlockSpec((tm, hw), lambda i: (i, 0)),
                  pl.BlockSpec((tm, hw), lambda i: (i, 0)),
                  pl.BlockSpec((None, hw, d), lambda i: (layer, 0, 0)),
                  pl.BlockSpec((None, hw, d), lambda i: (layer, 1, 0)),
                  pl.BlockSpec((tm, d), lambda i: (i, 0)),
                  pl.BlockSpec((None, 1, d), lambda i: (layer * 3, 0, 0)),
                  pl.BlockSpec((None, 1, d), lambda i: (layer * 3, 0, 0))],
        out_specs=pl.BlockSpec((tm, d), lambda i: (i, 0)),
        out_shape=jax.ShapeDtypeStruct((n, d), F32),
        compiler_params=_params(1),
        name="outproj_ln",
    )(y_hg, y_gd, w_out, w_out, x, ln_g, ln_b)


def _attn_router_kernel(x_ref, wq_ref, k_ref, v_ref, wo_ref, g_ref, b_ref, wr_ref, br_ref,
                        o_ref, route_ref):
    x = x_ref[...]
    q = _mm(x, wq_ref[...]).astype(BF16)
    heads = []
    for h in range(MEM_HEADS):
        sl = slice(h * MEM_DH, (h + 1) * MEM_DH)
        s = lax.dot_general(q[:, sl], k_ref[:, sl], NT_DIMS, preferred_element_type=F32)
        s = s * (MEM_DH ** -0.5)
        p = jnp.exp(s - jnp.max(s, axis=-1, keepdims=True))
        p = p / jnp.sum(p, axis=-1, keepdims=True)
        heads.append(_mm(p, v_ref[:, sl]))
    xa = _mm(jnp.concatenate(heads, axis=1), wo_ref[...])
    x2 = _layer_norm(ALPHA * x + xa, g_ref[...], b_ref[...])
    o_ref[...] = x2

    logits = _mm(x2, wr_ref[...]) + br_ref[...]
    lane = lax.broadcasted_iota(jnp.int32, (1, ROUTE_W), 1).astype(F32)
    neg = -jnp.inf
    first_at = lambda val, top: jnp.min(jnp.where(val == top, lane, float(ROUTE_W)), axis=-1,
                                        keepdims=True)
    gl = jnp.where(lane < N_GROUPS, logits, neg)
    g_max = jnp.max(gl, axis=-1, keepdims=True)
    g_sel = first_at(gl, g_max)
    g_gate = 1.0 / jnp.sum(jnp.exp(gl - g_max), axis=-1, keepdims=True)
    lo = N_GROUPS + EXPERTS_PER_GROUP * g_sel
    el = jnp.where((lane >= lo) & (lane < lo + EXPERTS_PER_GROUP), logits, neg)
    t1 = jnp.max(el, axis=-1, keepdims=True)
    i1 = first_at(el, t1)
    el2 = jnp.where(lane == i1, neg, el)
    t2 = jnp.max(el2, axis=-1, keepdims=True)
    i2 = first_at(el2, t2)
    e2 = jnp.exp(t2 - t1)
    w1 = g_gate / (1.0 + e2)
    w2 = g_gate * e2 / (1.0 + e2)
    route_ref[...] = jnp.where(lane == 0, i1 - N_GROUPS,
                               jnp.where(lane == 1, i2 - N_GROUPS,
                                         jnp.where(lane == 2, w1, jnp.where(lane == 3, w2, 0.0))))


def _attn_router(x, kv_all, w_mq, w_mo, w_route, b_route, layer, ln_g, ln_b, bsz, seq):
    n, d = x.shape
    tq = ROW_TILE
    nq = seq // tq
    mlen = kv_all.shape[0] // bsz
    wspec = lambda: pl.BlockSpec((None, d, d), lambda b, i: (layer, 0, 0))
    lnspec = lambda: pl.BlockSpec((None, 1, d), lambda b, i: (layer * 3 + 1, 0, 0))
    return pl.pallas_call(
        _attn_router_kernel,
        grid=(bsz, nq),
        in_specs=[pl.BlockSpec((tq, d), lambda b, i: (b * nq + i, 0)),
                  wspec(),
                  pl.BlockSpec((mlen, d), lambda b, i: (b, 2 * layer)),
                  pl.BlockSpec((mlen, d), lambda b, i: (b, 2 * layer + 1)),
                  wspec(), lnspec(), lnspec(),
                  pl.BlockSpec((None, d, ROUTE_W), lambda b, i: (layer, 0, 0)),
                  pl.BlockSpec((None, 1, ROUTE_W), lambda b, i: (layer, 0, 0))],
        out_specs=[pl.BlockSpec((tq, d), lambda b, i: (b * nq + i, 0)),
                   pl.BlockSpec((tq, ROUTE_W), lambda b, i: (b * nq + i, 0))],
        out_shape=[jax.ShapeDtypeStruct((n, d), F32), jax.ShapeDtypeStruct((n, ROUTE_W), F32)],
        compiler_params=_params(2),
        name="attn_router",
    )(x, w_mq, kv_all, kv_all, w_mo, ln_g, ln_b, w_route, b_route)


def _dispatch_kernel(dest_ref, zstart_ref, x_hbm, zeros_hbm, buf_hbm, sems, zsem):
    step = pl.program_id(0)
    nstep = pl.num_programs(0)
    rows = 2 * GATHER_TILE

    @pl.when(step == 0)
    def _():
        def zcopy(e):
            start = pl.multiple_of(jnp.maximum(zstart_ref[e], 0), EXPERT_ROWS)
            return pltpu.make_async_copy(zeros_hbm, buf_hbm.at[pl.ds(start, EXPERT_ROWS), :], zsem)

        def zstart(e, carry):
            @pl.when(zstart_ref[e] >= 0)
            def _():
                zcopy(e).start()
            return carry

        def zwait(e, carry):
            @pl.when(zstart_ref[e] >= 0)
            def _():
                zcopy(e).wait()
            return carry

        lax.fori_loop(0, zstart_ref.shape[0], zstart, 0)
        lax.fori_loop(0, zstart_ref.shape[0], zwait, 0)

    slot = step % 2

    def issue(i, carry):
        a = step * rows + i
        pltpu.make_async_copy(x_hbm.at[pl.ds(a // TOP_K, 1), :],
                              buf_hbm.at[pl.ds(dest_ref[a], 1), :], sems.at[slot]).start()
        return carry

    lax.fori_loop(0, rows, issue, 0, unroll=8)

    def wait_batch(s):
        pltpu.make_async_copy(x_hbm.at[pl.ds(0, rows), :], buf_hbm.at[pl.ds(0, rows), :],
                              sems.at[s]).wait()

    @pl.when(step > 0)
    def _():
        wait_batch(1 - slot)

    @pl.when(step == nstep - 1)
    def _():
        wait_batch(slot)


def _dispatch(x, dest, zstart, n_rows):
    n, d = x.shape
    m = dest.shape[0]
    any_spec = pl.BlockSpec(memory_space=pl.ANY)
    grid_spec = pltpu.PrefetchScalarGridSpec(
        num_scalar_prefetch=2, grid=(m // (2 * GATHER_TILE),),
        in_specs=[any_spec, any_spec], out_specs=any_spec,
        scratch_shapes=[pltpu.SemaphoreType.DMA((2,)), pltpu.SemaphoreType.DMA(())])
    return pl.pallas_call(
        _dispatch_kernel,
        grid_spec=grid_spec,
        out_shape=jax.ShapeDtypeStruct((n_rows, d), F32),
        compiler_params=_params(1),
        name="moe_dispatch",
    )(dest, zstart, x, jnp.zeros((EXPERT_ROWS, d), F32))


def _expert_kernel(be_ref, nu_ref, x_ref, wg_ref, wu_ref, wd_ref, o_ref, wg_bf, wu_bf, wd_bf):
    b = pl.program_id(0)
    changed = jnp.logical_or(b == 0, be_ref[b] != be_ref[jnp.maximum(b - 1, 0)])

    @pl.when(changed)
    def _():
        wg_bf[...] = wg_ref[...].astype(BF16)
        wu_bf[...] = wu_ref[...].astype(BF16)
        wd_bf[...] = wd_ref[...].astype(BF16)

    @pl.when(b < nu_ref[0])
    def _():
        xb = x_ref[...].astype(BF16)
        hg = jnp.dot(xb, wg_bf[...], preferred_element_type=F32)
        hu = jnp.dot(xb, wu_bf[...], preferred_element_type=F32)
        hid = hg * _sigmoid(hg) * hu
        o_ref[...] = jnp.dot(hid.astype(BF16), wd_bf[...], preferred_element_type=F32)

    @pl.when(b >= nu_ref[0])
    def _():
        o_ref[...] = jnp.zeros_like(o_ref)


def _experts(buf, block_expert, n_used, w_gate, w_up, w_down, layer):
    p, d = buf.shape
    nb = p // EXPERT_ROWS
    grid_spec = pltpu.PrefetchScalarGridSpec(
        num_scalar_prefetch=2, grid=(nb,),
        in_specs=[pl.BlockSpec((EXPERT_ROWS, d), lambda b, be, nu: (jnp.minimum(b, nu[0] - 1), 0)),
                  pl.BlockSpec((None, None, d, D_EXPERT), lambda b, be, nu: (layer, be[b], 0, 0)),
                  pl.BlockSpec((None, None, d, D_EXPERT), lambda b, be, nu: (layer, be[b], 0, 0)),
                  pl.BlockSpec((None, None, D_EXPERT, d), lambda b, be, nu: (layer, be[b], 0, 0))],
        out_specs=pl.BlockSpec((EXPERT_ROWS, d), lambda b, be, nu: (b, 0)),
        scratch_shapes=[pltpu.VMEM((d, D_EXPERT), BF16), pltpu.VMEM((d, D_EXPERT), BF16),
                        pltpu.VMEM((D_EXPERT, d), BF16)])
    return pl.pallas_call(
        _expert_kernel,
        grid_spec=grid_spec,
        out_shape=jax.ShapeDtypeStruct((p, d), F32),
        compiler_params=_params(1),
        name="moe_experts",
    )(block_expert, n_used, buf, w_gate, w_up, w_down)


def _combine_ln_kernel(dest_ref, y_hbm, x_ref, route_ref, g_ref, b_ref, o_ref, rows_ref, sems):
    step = pl.program_id(0)
    nstep = pl.num_programs(0)
    tm = x_ref.shape[0]

    def issue(s, slot):
        def body(i, carry):
            a = s * (TOP_K * tm) + TOP_K * i
            for kk in range(TOP_K):
                pltpu.make_async_copy(y_hbm.at[pl.ds(dest_ref[a + kk], 1), :],
                                      rows_ref.at[slot, kk, pl.ds(i, 1), :], sems.at[slot]).start()
            return carry
        lax.fori_loop(0, tm, body, 0, unroll=8)

    @pl.when(step == 0)
    def _():
        issue(0, 0)

    slot = step % 2

    @pl.when(step + 1 < nstep)
    def _():
        issue(step + 1, 1 - slot)

    for kk in range(TOP_K):
        pltpu.make_async_copy(y_hbm.at[pl.ds(0, tm), :], rows_ref.at[slot, kk], sems.at[slot]).wait()
    route = route_ref[...]
    ff = route[:, 2:3] * rows_ref[slot, 0] + route[:, 3:4] * rows_ref[slot, 1]
    o_ref[...] = _layer_norm(ALPHA * x_ref[...] + ff, g_ref[...], b_ref[...])


def _combine_ln(ybuf, dest, x, route, layer, ln_g, ln_b):
    n, d = x.shape
    tm = GATHER_TILE
    lnspec = lambda: pl.BlockSpec((None, 1, d), lambda i, dest: (layer * 3 + 2, 0, 0))
    grid_spec = pltpu.PrefetchScalarGridSpec(
        num_scalar_prefetch=1, grid=(n // tm,),
        in_specs=[pl.BlockSpec(memory_space=pl.ANY),
                  pl.BlockSpec((tm, d), lambda i, dest: (i, 0)),
                  pl.BlockSpec((tm, ROUTE_W), lambda i, dest: (i, 0)),
                  lnspec(), lnspec()],
        out_specs=pl.BlockSpec((tm, d), lambda i, dest: (i, 0)),
        scratch_shapes=[pltpu.VMEM((2, TOP_K, tm, d), F32), pltpu.SemaphoreType.DMA((2,))])
    return pl.pallas_call(
        _combine_ln_kernel,
        grid_spec=grid_spec,
        out_shape=jax.ShapeDtypeStruct((n, d), F32),
        compiler_params=_params(1),
        name="moe_combine_ln",
    )(dest, ybuf, x, route, ln_g, ln_b)


def _routing_tables(route, n_tok):
    m = n_tok * TOP_K
    flat_e = route[:, 0:TOP_K].astype(jnp.int32).reshape(m)
    onehot = (flat_e[:, None] == jnp.arange(N_EXPERTS, dtype=jnp.int32)[None, :]).astype(jnp.int32)
    csum = jnp.cumsum(onehot, axis=0)
    counts = csum[-1]
    rank = jnp.take_along_axis(csum, flat_e[:, None], axis=1)[:, 0] - 1
    pcounts = (counts + EXPERT_ROWS - 1) // EXPERT_ROWS * EXPERT_ROWS
    pends = jnp.cumsum(pcounts)
    dest = (pends - pcounts)[flat_e] + rank
    n_blocks = m // EXPERT_ROWS + N_EXPERTS
    n_used = pends[-1] // EXPERT_ROWS
    blk = jnp.minimum(jnp.arange(n_blocks, dtype=jnp.int32), n_used - 1)
    block_expert = jnp.minimum(jnp.searchsorted(pends, blk * EXPERT_ROWS, side='right'),
                               N_EXPERTS - 1).astype(jnp.int32)
    all_blk = jnp.arange(n_blocks, dtype=jnp.int32)
    zstart = jnp.concatenate([jnp.where(counts > 0, pends - EXPERT_ROWS, -1),
                              jnp.where(all_blk >= n_used, all_blk * EXPERT_ROWS, -1)]).astype(jnp.int32)
    return dest.astype(jnp.int32), zstart, block_expert, n_used.reshape(1).astype(jnp.int32), n_blocks


def kernel(x, mem, w_in, hg_lb_logits, hg_norm_w, gd_conv_w, gd_a_log, gd_dt_bias, gd_norm_w, w_out,
           mem_ln_g, mem_ln_b, w_mq, w_mk, w_mv, w_mo, w_group, b_group, w_router, b_router,
           w_gate, w_up, w_down, ln_g, ln_b):
    bsz, seq, d = x.shape
    n = bsz * seq
    lb_all = jnp.cumsum(jax.nn.softmax(hg_lb_logits.astype(F32), axis=0), axis=0)
    lb_all = lb_all - lb_all[0]
    lb_tab = jnp.stack([jnp.log(lb_all), jnp.log1p(-lb_all), 1.0 - lb_all], axis=1)
    sp = np.cumsum((512, 512, 512, 512, 512, 512, 512, GD_HEADS, GD_HEADS, 512))
    seg = lambda a, b: w_in[:, :, (sp[a - 1] if a else 0):sp[b]]
    w_in_r = jnp.concatenate(
        [seg(0, 6), seg(9, 9), seg(7, 8), jnp.zeros((DEPTH, d, HEAD_W - 2 * GD_HEADS), F32)],
        axis=-1).astype(BF16)
    w_out_b = w_out.astype(BF16)
    w_mq_b, w_mo_b = w_mq.astype(BF16), w_mo.astype(BF16)
    w_kv = jnp.stack([w_mk, w_mv], axis=1).astype(BF16)
    w_kv = w_kv.transpose(2, 0, 1, 3).reshape(d, DEPTH * 2 * d)
    pad = ROUTE_W - N_GROUPS - N_EXPERTS
    w_route = jnp.concatenate([w_group, w_router, jnp.zeros((DEPTH, d, pad), F32)], axis=-1).astype(BF16)
    b_route = jnp.concatenate([b_group, b_router, jnp.zeros((DEPTH, pad), F32)], axis=-1)
    b_route = b_route.reshape(DEPTH, 1, ROUTE_W).astype(F32)
    ln_g3 = ln_g.reshape(DEPTH * 3, 1, d)
    ln_b3 = ln_b.reshape(DEPTH * 3, 1, d)

    mem_n = _ln(mem.reshape(-1, d), mem_ln_g, mem_ln_b, ROW_TILE)
    kv_all = _matmul(mem_n, w_kv, BF16, ROW_TILE, 2 * d)

    xt = x.reshape(n, d)
    for l in range(DEPTH):
        proj = _matmul(xt, w_in_r[l], F32, ROW_TILE // 2, PROJ_W)
        y_hg = _hgrn(proj, lb_tab[l], hg_norm_w[l], bsz, seq)
        y_gd = _gdn(proj, gd_conv_w[l], gd_a_log[l], gd_dt_bias[l], gd_norm_w[l], bsz, seq)
        x1 = _outproj_ln(y_hg, y_gd, w_out_b, l, xt, ln_g3, ln_b3)
        x2, route = _attn_router(x1, kv_all, w_mq_b, w_mo_b, w_route, b_route, l, ln_g3, ln_b3,
                                 bsz, seq)
        dest, zstart, block_expert, n_used, n_blocks = _routing_tables(route, n)
        buf = _dispatch(x2, dest, zstart, n_blocks * EXPERT_ROWS)
        ybuf = _experts(buf, block_expert, n_used, w_gate, w_up, w_down, l)
        xt = _combine_ln(ybuf, dest, x2, route, l, ln_g3, ln_b3)
    return xt.reshape(bsz, seq, d)
```

```python
import functools
import math

import jax
import jax.numpy as jnp
import numpy as np
from jax import lax
from jax.experimental import pallas as pl
from jax.experimental.pallas import tpu as pltpu

D_MODEL = 1024
DEPTH = 4
HG_HEADS = 4
HG_DK = 128
GD_HEADS = 4
GD_DK = 128
HEAD_W = 128
CONV_K = 4
MEM_HEADS = 4
MEM_DH = D_MODEL // MEM_HEADS
N_GROUPS = 4
EXPERTS_PER_GROUP = 8
N_EXPERTS = N_GROUPS * EXPERTS_PER_GROUP
TOP_K = 2
D_EXPERT = 512
ALPHA = (2.0 * DEPTH) ** 0.25
LN_EPS = 1e-5
RMS_EPS = 1e-6
L2_EPS = 1e-6

COL_HQ, COL_HF, COL_HI, COL_HZ, COL_GQ, COL_GK, COL_GV, COL_GZ, COL_GAB = 0, 4, 8, 12, 16, 20, 24, 28, 32
PROJ_W = 33 * HEAD_W

TIME_BLOCK = 256
GD_CHUNK = 64
ROW_TILE = 512
EXPERT_ROWS = 256
ROUTE_W = 128
VMEM_LIMIT_BYTES = 48 * 1024 * 1024

BF16 = jnp.bfloat16
F32 = jnp.float32
NT_DIMS = (((1,), (1,)), ((), ()))
TN_DIMS = (((0,), (0,)), ((), ()))


def _params(n_grid):
    return pltpu.CompilerParams(dimension_semantics=("arbitrary",) * n_grid,
                                vmem_limit_bytes=VMEM_LIMIT_BYTES)


def _mm(a, b):
    return jnp.dot(a.astype(BF16), b.astype(BF16), preferred_element_type=F32)


def _mm_nt(a, b):
    return lax.dot_general(a.astype(BF16), b.astype(BF16), NT_DIMS, preferred_element_type=F32)


def _mm_tn(a, b):
    return lax.dot_general(a.astype(BF16), b.astype(BF16), TN_DIMS, preferred_element_type=F32)


def _cumsum_rows(tri, x):
    p0 = x.astype(BF16)
    r0 = x - p0.astype(F32)
    p1 = r0.astype(BF16)
    p2 = (r0 - p1.astype(F32)).astype(BF16)
    dot = lambda p: jnp.dot(tri, p, preferred_element_type=F32)
    return dot(p0) + dot(p1) + dot(p2)


def _sigmoid(x):
    e = jnp.exp(-jnp.abs(x))
    return jnp.where(x >= 0, 1.0, e) / (1.0 + e)


def _softplus(x):
    return jnp.maximum(x, 0.0) + jnp.log1p(jnp.exp(-jnp.abs(x)))


def _layer_norm(z, g, b):
    mu = jnp.mean(z, axis=-1, keepdims=True)
    zc = z - mu
    var = jnp.mean(zc * zc, axis=-1, keepdims=True)
    return zc * lax.rsqrt(var + LN_EPS) * g + b


def _matmul_kernel(x_ref, w_ref, o_ref):
    o_ref[...] = _mm(x_ref[...], w_ref[...]).astype(o_ref.dtype)


def _matmul(x, w, out_dtype, tm, tn):
    m, k = x.shape
    n = w.shape[1]
    return pl.pallas_call(
        _matmul_kernel,
        grid=(m // tm, n // tn),
        in_specs=[pl.BlockSpec((tm, k), lambda i, j: (i, 0)),
                  pl.BlockSpec((k, tn), lambda i, j: (0, j))],
        out_specs=pl.BlockSpec((tm, tn), lambda i, j: (i, j)),
        out_shape=jax.ShapeDtypeStruct((m, n), out_dtype),
        compiler_params=_params(2),
        name="matmul",
    )(x, w)


def _ln_kernel(x_ref, g_ref, b_ref, o_ref):
    o_ref[...] = _layer_norm(x_ref[...], g_ref[...], b_ref[...])


def _ln(x, g, b, tm):
    n, d = x.shape
    return pl.pallas_call(
        _ln_kernel,
        grid=(n // tm,),
        in_specs=[pl.BlockSpec((tm, d), lambda i: (i, 0)),
                  pl.BlockSpec((1, d), lambda i: (0, 0)),
                  pl.BlockSpec((1, d), lambda i: (0, 0))],
        out_specs=pl.BlockSpec((tm, d), lambda i: (i, 0)),
        out_shape=jax.ShapeDtypeStruct((n, d), F32),
        compiler_params=_params(1),
        name="mem_ln",
    )(x, g.reshape(1, d), b.reshape(1, d))


def _ref_rows(cum, row, h):
    c, w = cum.shape
    if 2 * h <= 8:
        j = row & (2 * h - 1)
        m = cum
        for dl in range(-(h - 1), h + 1):
            if dl != 0:
                m = jnp.where(j - (h - 1) == dl, pltpu.roll(cum, dl % c, axis=0), m)
        return m
    pieces = [jnp.broadcast_to(cum[s + h - 1:s + h, :], (2 * h, w)) for s in range(0, c, 2 * h)]
    return jnp.concatenate(pieces, axis=0) if len(pieces) > 1 else pieces[0]


def _hgrn_kernel(q_ref, f_ref, i_ref, z_ref, lb_ref, nw_ref, tri_ref, o_ref, st_ref):
    @pl.when(pl.program_id(2) == 0)
    def _():
        st_ref[...] = jnp.zeros_like(st_ref)

    c = q_ref.shape[0]
    q, zf, v = q_ref[...], f_ref[...], i_ref[...]
    log_lb, log1m_lb, one_m_lb = lb_ref[0:1, :], lb_ref[1:2, :], lb_ref[2:3, :]
    e = jnp.exp(-jnp.abs(zf))
    log_sig = jnp.minimum(zf, 0.0) - jnp.log1p(e)
    b = log1m_lb + log_sig
    log_f = jnp.maximum(log_lb, b) + jnp.log1p(jnp.exp(-jnp.abs(log_lb - b)))
    k = one_m_lb * (jnp.where(zf >= 0, e, 1.0) / (1.0 + e))
    cum = _cumsum_rows(tri_ref[...], log_f)

    row = lax.broadcasted_iota(jnp.int32, (c, 1), 0)
    col = lax.broadcasted_iota(jnp.int32, (1, c), 1)
    scores = jnp.where(row == col, jnp.sum(q * k, axis=-1, keepdims=True), 0.0)
    h = 1
    while h < c:
        x = jnp.exp(-jnp.abs(cum - _ref_rows(cum, row, h)))
        upper = (row & (2 * h - 1)) >= h
        sc = _mm_nt(jnp.where(upper, q * x, 0.0), jnp.where(upper, 0.0, k * x))
        if 2 * h < c:
            shift = int(math.log2(2 * h))
            sc = jnp.where((row >> shift) == (col >> shift), sc, 0.0)
        scores = scores + sc
        h *= 2

    st = st_ref[...]
    o = _mm_nt(q * jnp.exp(cum), st) + _mm(scores, v)
    cum_last = cum[c - 1:c, :]
    st_ref[...] = jnp.exp(cum_last) * st + _mm_tn(v, k * jnp.exp(cum_last - cum))
    o = o * lax.rsqrt(jnp.mean(o * o, axis=-1, keepdims=True) + RMS_EPS) * nw_ref[...]
    o_ref[...] = (o * _sigmoid(z_ref[...])).astype(o_ref.dtype)


def _hgrn(proj, lb_tab, norm_w, bsz, seq):
    nt = seq // TIME_BLOCK
    tri = jnp.asarray(np.tril(np.ones((TIME_BLOCK, TIME_BLOCK), np.float32)), BF16)
    col = lambda c0: pl.BlockSpec((TIME_BLOCK, HEAD_W), lambda b, h, t: (b * nt + t, c0 + h))
    return pl.pallas_call(
        _hgrn_kernel,
        grid=(bsz, HG_HEADS, nt),
        in_specs=[col(COL_HQ), col(COL_HF), col(COL_HI), col(COL_HZ),
                  pl.BlockSpec((3, HEAD_W), lambda b, h, t: (0, h)),
                  pl.BlockSpec((1, HEAD_W), lambda b, h, t: (0, 0)),
                  pl.BlockSpec((TIME_BLOCK, TIME_BLOCK), lambda b, h, t: (0, 0))],
        out_specs=pl.BlockSpec((TIME_BLOCK, HEAD_W), lambda b, h, t: (b * nt + t, h)),
        out_shape=jax.ShapeDtypeStruct((bsz * seq, HG_HEADS * HEAD_W), BF16),
        scratch_shapes=[pltpu.VMEM((HEAD_W, HG_DK), F32)],
        compiler_params=_params(3),
        name="hgrn2",
    )(proj, proj, proj, proj, lb_tab, norm_w.reshape(1, HEAD_W), tri)


def _conv_silu(x_ref, w_ref, buf_ref, first):
    tb = x_ref.shape[0]

    @pl.when(first)
    def _():
        buf_ref[0:8, :] = jnp.zeros((8, buf_ref.shape[1]), F32)

    @pl.when(jnp.logical_not(first))
    def _():
        buf_ref[0:8, :] = buf_ref[tb:tb + 8, :]

    buf_ref[8:tb + 8, :] = x_ref[...]
    y = w_ref[CONV_K - 1:CONV_K, :] * x_ref[...]
    for j in range(1, CONV_K):
        y = y + w_ref[CONV_K - 1 - j:CONV_K - j, :] * buf_ref[8 - j:8 - j + tb, :]
    return y * _sigmoid(y)


def _gdn_kernel(alog_ref, dtb_ref, q_ref, k_ref, v_ref, z_ref, ab_ref, wq_ref, wk_ref, wv_ref,
                nw_ref, tri_ref, o_ref, s_ref, bq_ref, bk_ref, bv_ref):
    head = pl.program_id(1)
    first = pl.program_id(2) == 0

    @pl.when(first)
    def _():
        s_ref[...] = jnp.zeros_like(s_ref)

    tb = q_ref.shape[0]
    cq = _conv_silu(q_ref, wq_ref, bq_ref, first)
    ck = _conv_silu(k_ref, wk_ref, bk_ref, first)
    v = _conv_silu(v_ref, wv_ref, bv_ref, first)
    q = cq * lax.rsqrt(jnp.sum(cq * cq, axis=-1, keepdims=True) + L2_EPS) * (GD_DK ** -0.5)
    k = ck * lax.rsqrt(jnp.sum(ck * ck, axis=-1, keepdims=True) + L2_EPS)

    lane = lax.broadcasted_iota(jnp.int32, (1, HEAD_W), 1)
    ab = ab_ref[...]
    ga = jnp.sum(jnp.where(lane == head, ab, 0.0), axis=-1, keepdims=True)
    gb = jnp.sum(jnp.where(lane == GD_HEADS + head, ab, 0.0), axis=-1, keepdims=True)
    neg_a = -jnp.exp(jnp.full((1, 1), alog_ref[head], F32))
    g = neg_a * _softplus(ga + dtb_ref[head])
    beta = _sigmoid(gb)
    lc = _cumsum_rows(tri_ref[...], jnp.broadcast_to(g, (tb, HEAD_W)))
    lc_col = lc[:, 0:1]
    lc_row = jnp.transpose(lc)[0:1, :]

    row = lax.broadcasted_iota(jnp.int32, (tb, 1), 0)
    col = lax.broadcasted_iota(jnp.int32, (1, tb), 1)
    shift = int(math.log2(GD_CHUNK))
    same = (row >> shift) == (col >> shift)
    decay = jnp.exp(jnp.where(same & (col <= row), lc_col - lc_row, -jnp.inf))
    kb = k * beta
    p = _mm_nt(jnp.concatenate([q, kb], axis=0), k)
    qk = p[:tb] * decay
    a = jnp.where(same & (col < row), p[tb:] * decay, 0.0)
    t = jnp.where(row == col, 1.0, 0.0) - a
    pw = a
    for _ in range(shift - 1):
        pw = _mm(pw, pw)
        t = t + _mm(t, pw)
    e_lc = jnp.exp(lc_col)
    uw = _mm(t, jnp.concatenate([v * beta, kb * e_lc], axis=1))
    u, w = uw[:, :HEAD_W], uw[:, HEAD_W:]
    qe = q * e_lc

    s = s_ref[...]
    outs = []
    for c0 in range(0, tb, GD_CHUNK):
        c1 = c0 + GD_CHUNK
        r = _mm(jnp.concatenate([w[c0:c1], qe[c0:c1]], axis=0), s)
        v_new = u[c0:c1] - r[:GD_CHUNK]
        pads = ([jnp.zeros((c0, HEAD_W), F32)] if c0 else []) + [v_new] + (
            [jnp.zeros((tb - c1, HEAD_W), F32)] if c1 < tb else [])
        outs.append(r[GD_CHUNK:] + _mm(qk[c0:c1, :], jnp.concatenate(pads, axis=0)))
        lc_last = lc[c1 - 1:c1, :]
        kd = k[c0:c1] * jnp.exp(lc_last[:, 0:1] - lc_col[c0:c1])
        s = jnp.exp(lc_last) * s + _mm_tn(kd, v_new)
    s_ref[...] = s
    o = jnp.concatenate(outs, axis=0)
    o = o * lax.rsqrt(jnp.mean(o * o, axis=-1, keepdims=True) + RMS_EPS) * nw_ref[...]
    zg = z_ref[...]
    o_ref[...] = (o * (zg * _sigmoid(zg))).astype(o_ref.dtype)


def _gdn(proj, conv_w, a_log, dt_bias, norm_w, bsz, seq):
    nt = seq // TIME_BLOCK
    idx = np.arange(TIME_BLOCK)
    tri = (idx[:, None] >= idx[None, :]) & (idx[:, None] // GD_CHUNK == idx[None, :] // GD_CHUNK)
    tri = jnp.asarray(tri.astype(np.float32), BF16)
    col = lambda c0: pl.BlockSpec((TIME_BLOCK, HEAD_W), lambda b, h, t: (b * nt + t, c0 + h))
    cw = lambda c0: pl.BlockSpec((CONV_K, HEAD_W), lambda b, h, t: (0, c0 + h))
    smem = pl.BlockSpec(memory_space=pltpu.SMEM)
    return pl.pallas_call(
        _gdn_kernel,
        grid=(bsz, GD_HEADS, nt),
        in_specs=[smem, smem, col(COL_GQ), col(COL_GK), col(COL_GV), col(COL_GZ),
                  pl.BlockSpec((TIME_BLOCK, HEAD_W), lambda b, h, t: (b * nt + t, COL_GAB)),
                  cw(0), cw(GD_HEADS), cw(2 * GD_HEADS),
                  pl.BlockSpec((1, HEAD_W), lambda b, h, t: (0, 0)),
                  pl.BlockSpec((TIME_BLOCK, TIME_BLOCK), lambda b, h, t: (0, 0))],
        out_specs=pl.BlockSpec((TIME_BLOCK, HEAD_W), lambda b, h, t: (b * nt + t, h)),
        out_shape=jax.ShapeDtypeStruct((bsz * seq, GD_HEADS * HEAD_W), BF16),
        scratch_shapes=[pltpu.VMEM((GD_DK, HEAD_W), F32)] + [
            pltpu.VMEM((TIME_BLOCK + 8, HEAD_W), F32) for _ in range(3)],
        compiler_params=_params(3),
        name="gated_deltanet",
    )(a_log, dt_bias, proj, proj, proj, proj, proj, conv_w, conv_w, conv_w,
      norm_w.reshape(1, HEAD_W), tri)


def _outproj_ln_kernel(yh_ref, yg_ref, wh_ref, wg_ref, x_ref, g_ref, b_ref, o_ref):
    mix = (jnp.dot(yh_ref[...], wh_ref[...], preferred_element_type=F32)
           + jnp.dot(yg_ref[...], wg_ref[...], preferred_element_type=F32))
    o_ref[...] = _layer_norm(ALPHA * x_ref[...] + mix, g_ref[...], b_ref[...])


def _outproj_ln(y_hg, y_gd, w_out, layer, x, ln_g, ln_b):
    n, d = x.shape
    hw = y_hg.shape[1]
    tm = ROW_TILE
    return pl.pallas_call(
        _outproj_ln_kernel,
        grid=(n // tm,),
        in_specs=[pl.BlockSpec((tm, hw), lambda i: (i, 0)),
                  pl.BlockSpec((tm, hw), lambda i: (i, 0)),
                  pl.BlockSpec((None, hw, d), lambda i: (layer, 0, 0)),
                  pl.BlockSpec((None, hw, d), lambda i: (layer, 1, 0)),
                  pl.BlockSpec((tm, d), lambda i: (i, 0)),
                  pl.BlockSpec((None, 1, d), lambda i: (layer * 3, 0, 0)),
                  pl.BlockSpec((None, 1, d), lambda i: (layer * 3, 0, 0))],
        out_specs=pl.BlockSpec((tm, d), lambda i: (i, 0)),
        out_shape=jax.ShapeDtypeStruct((n, d), F32),
        compiler_params=_params(1),
        name="outproj_ln",
    )(y_hg, y_gd, w_out, w_out, x, ln_g, ln_b)


def _attn_router_kernel(x_ref, wq_ref, k_ref, v_ref, wo_ref, g_ref, b_ref, wr_ref, br_ref,
                        o_ref, route_ref):
    x = x_ref[...]
    q = _mm(x, wq_ref[...]).astype(BF16)
    heads = []
    for h in range(MEM_HEADS):
        sl = slice(h * MEM_DH, (h + 1) * MEM_DH)
        s = lax.dot_general(q[:, sl], k_ref[:, sl], NT_DIMS, preferred_element_type=F32)
        s = s * (MEM_DH ** -0.5)
        p = jnp.exp(s - jnp.max(s, axis=-1, keepdims=True))
        p = p / jnp.sum(p, axis=-1, keepdims=True)
        heads.append(_mm(p, v_ref[:, sl]))
    xa = _mm(jnp.concatenate(heads, axis=1), wo_ref[...])
    x2 = _layer_norm(ALPHA * x + xa, g_ref[...], b_ref[...])
    o_ref[...] = x2

    logits = _mm(x2, wr_ref[...]) + br_ref[...]
    lane = lax.broadcasted_iota(jnp.int32, (1, ROUTE_W), 1).astype(F32)
    neg = -jnp.inf
    first_at = lambda val, top: jnp.min(jnp.where(val == top, lane, float(ROUTE_W)), axis=-1,
                                        keepdims=True)
    gl = jnp.where(lane < N_GROUPS, logits, neg)
    g_max = jnp.max(gl, axis=-1, keepdims=True)
    g_sel = first_at(gl, g_max)
    g_gate = 1.0 / jnp.sum(jnp.exp(gl - g_max), axis=-1, keepdims=True)
    lo = N_GROUPS + EXPERTS_PER_GROUP * g_sel
    el = jnp.where((lane >= lo) & (lane < lo + EXPERTS_PER_GROUP), logits, neg)
    t1 = jnp.max(el, axis=-1, keepdims=True)
    i1 = first_at(el, t1)
    el2 = jnp.where(lane == i1, neg, el)
    t2 = jnp.max(el2, axis=-1, keepdims=True)
    i2 = first_at(el2, t2)
    e2 = jnp.exp(t2 - t1)
    w1 = g_gate / (1.0 + e2)
    w2 = g_gate * e2 / (1.0 + e2)
    route_ref[...] = jnp.where(lane == 0, i1 - N_GROUPS,
                               jnp.where(lane == 1, i2 - N_GROUPS,
                                         jnp.where(lane == 2, w1, jnp.where(lane == 3, w2, 0.0))))


def _attn_router(x, kv_all, w_mq, w_mo, w_route, b_route, layer, ln_g, ln_b, bsz, seq):
    n, d = x.shape
    tq = ROW_TILE
    nq = seq // tq
    mlen = kv_all.shape[0] // bsz
    wspec = lambda: pl.BlockSpec((None, d, d), lambda b, i: (layer, 0, 0))
    lnspec = lambda: pl.BlockSpec((None, 1, d), lambda b, i: (layer * 3 + 1, 0, 0))
    return pl.pallas_call(
        _attn_router_kernel,
        grid=(bsz, nq),
        in_specs=[pl.BlockSpec((tq, d), lambda b, i: (b * nq + i, 0)),
                  wspec(),
                  pl.BlockSpec((mlen, d), lambda b, i: (b, 2 * layer)),
                  pl.BlockSpec((mlen, d), lambda b, i: (b, 2 * layer + 1)),
                  wspec(), lnspec(), lnspec(),
                  pl.BlockSpec((None, d, ROUTE_W), lambda b, i: (layer, 0, 0)),
                  pl.BlockSpec((None, 1, ROUTE_W), lambda b, i: (layer, 0, 0))],
        out_specs=[pl.BlockSpec((tq, d), lambda b, i: (b * nq + i, 0)),
                   pl.BlockSpec((tq, ROUTE_W), lambda b, i: (b * nq + i, 0))],
        out_shape=[jax.ShapeDtypeStruct((n, d), F32), jax.ShapeDtypeStruct((n, ROUTE_W), F32)],
        compiler_params=_params(2),
        name="attn_router",
    )(x, w_mq, kv_all, kv_all, w_mo, ln_g, ln_b, w_route, b_route)


def _expert_kernel(order_ref, bstart_ref, bend_ref, be_ref, nu_ref, x_hbm, zeros_hbm, wg_ref, wu_ref,
                   wd_ref, y_hbm, rows_ref, yv_ref, gsem, ssem, zsem, wg_bf, wu_bf, wd_bf):
    b = pl.program_id(0)
    nu = nu_ref[0]
    m = order_ref.shape[0]
    n_tok = m // TOP_K
    slot = b % 2

    def assignment(blk, i):
        idx = bstart_ref[blk] + i
        return idx, order_ref[jnp.minimum(idx, m - 1)]

    def gather(blk, s):
        def body(i, carry):
            _, a = assignment(blk, i)
            pltpu.make_async_copy(x_hbm.at[pl.ds(a // TOP_K, 1), :], rows_ref.at[s, pl.ds(i, 1), :],
                                  gsem.at[s]).start()
            return carry
        lax.fori_loop(0, EXPERT_ROWS, body, 0, unroll=8)

    def scatter(blk, s):
        def body(i, carry):
            idx, a = assignment(blk, i)
            row = jnp.where(idx < bend_ref[blk], (a % TOP_K) * n_tok + a // TOP_K, m + i)
            pltpu.make_async_copy(yv_ref.at[s, pl.ds(i, 1), :], y_hbm.at[pl.ds(row, 1), :],
                                  ssem.at[s]).start()
            return carry
        lax.fori_loop(0, EXPERT_ROWS, body, 0, unroll=8)

    def wait_gather(s):
        pltpu.make_async_copy(x_hbm.at[pl.ds(0, EXPERT_ROWS), :], rows_ref.at[s], gsem.at[s]).wait()

    def wait_scatter(s):
        pltpu.make_async_copy(yv_ref.at[s], y_hbm.at[pl.ds(0, EXPERT_ROWS), :], ssem.at[s]).wait()

    @pl.when(b == 0)
    def _():
        zc = pltpu.make_async_copy(zeros_hbm, y_hbm.at[pl.ds(m, EXPERT_ROWS), :], zsem)
        zc.start()
        zc.wait()
        gather(0, 0)

    @pl.when(b + 1 < nu)
    def _():
        gather(b + 1, 1 - slot)

    @pl.when(jnp.logical_or(b == 0, be_ref[b] != be_ref[jnp.maximum(b - 1, 0)]))
    def _():
        wg_bf[...] = wg_ref[...].astype(BF16)
        wu_bf[...] = wu_ref[...].astype(BF16)
        wd_bf[...] = wd_ref[...].astype(BF16)

    @pl.when(b < nu)
    def _():
        wait_gather(slot)
        xb = rows_ref[slot].astype(BF16)
        hg = jnp.dot(xb, wg_bf[...], preferred_element_type=F32)
        hu = jnp.dot(xb, wu_bf[...], preferred_element_type=F32)
        hid = hg * _sigmoid(hg) * hu
        y = jnp.dot(hid.astype(BF16), wd_bf[...], preferred_element_type=F32)

        @pl.when(b >= 2)
        def _():
            wait_scatter(slot)

        yv_ref[slot] = y
        scatter(b, slot)

        @pl.when(b == nu - 1)
        def _():
            wait_scatter(slot)

            @pl.when(b >= 1)
            def _():
                wait_scatter(1 - slot)


def _experts(x, order, bstart, bend, block_expert, n_used, w_gate, w_up, w_down, layer):
    n, d = x.shape
    m = order.shape[0]
    nb = m // EXPERT_ROWS + N_EXPERTS
    any_spec = pl.BlockSpec(memory_space=pl.ANY)
    wspec = lambda r, c: pl.BlockSpec((None, None, r, c),
                                      lambda b, o, bs, bn, be, nu: (layer, be[b], 0, 0))
    grid_spec = pltpu.PrefetchScalarGridSpec(
        num_scalar_prefetch=5, grid=(nb,),
        in_specs=[any_spec, any_spec, wspec(d, D_EXPERT), wspec(d, D_EXPERT), wspec(D_EXPERT, d)],
        out_specs=any_spec,
        scratch_shapes=[pltpu.VMEM((2, EXPERT_ROWS, d), F32), pltpu.VMEM((2, EXPERT_ROWS, d), F32),
                        pltpu.SemaphoreType.DMA((2,)), pltpu.SemaphoreType.DMA((2,)),
                        pltpu.SemaphoreType.DMA(()),
                        pltpu.VMEM((d, D_EXPERT), BF16), pltpu.VMEM((d, D_EXPERT), BF16),
                        pltpu.VMEM((D_EXPERT, d), BF16)])
    return pl.pallas_call(
        _expert_kernel,
        grid_spec=grid_spec,
        out_shape=jax.ShapeDtypeStruct((m + EXPERT_ROWS, d), F32),
        compiler_params=_params(1),
        name="moe_experts",
    )(order, bstart, bend, block_expert, n_used, x, jnp.zeros((EXPERT_ROWS, d), F32),
      w_gate, w_up, w_down)


def _combine_ln_kernel(y0_ref, y1_ref, x_ref, route_ref, g_ref, b_ref, o_ref):
    route = route_ref[...]
    ff = route[:, 2:3] * y0_ref[...] + route[:, 3:4] * y1_ref[...]
    o_ref[...] = _layer_norm(ALPHA * x_ref[...] + ff, g_ref[...], b_ref[...])


def _combine_ln(y, x, route, layer, ln_g, ln_b):
    n, d = x.shape
    tm = ROW_TILE
    lnspec = lambda: pl.BlockSpec((None, 1, d), lambda i: (layer * 3 + 2, 0, 0))
    return pl.pallas_call(
        _combine_ln_kernel,
        grid=(n // tm,),
        in_specs=[pl.BlockSpec((tm, d), lambda i: (i, 0)),
                  pl.BlockSpec((tm, d), lambda i: (n // tm + i, 0)),
                  pl.BlockSpec((tm, d), lambda i: (i, 0)),
                  pl.BlockSpec((tm, ROUTE_W), lambda i: (i, 0)),
                  lnspec(), lnspec()],
        out_specs=pl.BlockSpec((tm, d), lambda i: (i, 0)),
        out_shape=jax.ShapeDtypeStruct((n, d), F32),
        compiler_params=_params(1),
        name="moe_combine_ln",
    )(y, y, x, route, ln_g, ln_b)


def _routing_tables(route, n_tok):
    m = n_tok * TOP_K
    flat_e = route[:, 0:TOP_K].astype(jnp.int32).reshape(m)
    order = jnp.argsort(flat_e).astype(jnp.int32)
    experts = jnp.arange(N_EXPERTS, dtype=jnp.int32)
    counts = jnp.sum((flat_e[:, None] == experts[None, :]).astype(jnp.int32), axis=0)
    starts = jnp.cumsum(counts) - counts
    pcounts = (counts + EXPERT_ROWS - 1) // EXPERT_ROWS * EXPERT_ROWS
    pends = jnp.cumsum(pcounts)
    n_blocks = m // EXPERT_ROWS + N_EXPERTS
    n_used = pends[-1] // EXPERT_ROWS
    blk = jnp.minimum(jnp.arange(n_blocks, dtype=jnp.int32), n_used - 1)
    block_expert = jnp.minimum(jnp.searchsorted(pends, blk * EXPERT_ROWS, side='right'),
                               N_EXPERTS - 1).astype(jnp.int32)
    bstart = starts[block_expert] + blk * EXPERT_ROWS - (pends - pcounts)[block_expert]
    bend = (starts + counts)[block_expert]
    return (order, bstart.astype(jnp.int32), bend.astype(jnp.int32), block_expert,
            n_used.reshape(1).astype(jnp.int32))


def kernel(x, mem, w_in, hg_lb_logits, hg_norm_w, gd_conv_w, gd_a_log, gd_dt_bias, gd_norm_w, w_out,
           mem_ln_g, mem_ln_b, w_mq, w_mk, w_mv, w_mo, w_group, b_group, w_router, b_router,
           w_gate, w_up, w_down, ln_g, ln_b):
    bsz, seq, d = x.shape
    n = bsz * seq
    lb_all = jnp.cumsum(jax.nn.softmax(hg_lb_logits.astype(F32), axis=0), axis=0)
    lb_all = lb_all - lb_all[0]
    lb_tab = jnp.stack([jnp.log(lb_all), jnp.log1p(-lb_all), 1.0 - lb_all], axis=1)
    sp = np.cumsum((512, 512, 512, 512, 512, 512, 512, GD_HEADS, GD_HEADS, 512))
    seg = lambda a, b: w_in[:, :, (sp[a - 1] if a else 0):sp[b]]
    w_in_r = jnp.concatenate(
        [seg(0, 6), seg(9, 9), seg(7, 8), jnp.zeros((DEPTH, d, HEAD_W - 2 * GD_HEADS), F32)],
        axis=-1).astype(BF16)
    w_out_b = w_out.astype(BF16)
    w_mq_b, w_mo_b = w_mq.astype(BF16), w_mo.astype(BF16)
    w_kv = jnp.stack([w_mk, w_mv], axis=1).astype(BF16)
    w_kv = w_kv.transpose(2, 0, 1, 3).reshape(d, DEPTH * 2 * d)
    pad = ROUTE_W - N_GROUPS - N_EXPERTS
    w_route = jnp.concatenate([w_group, w_router, jnp.zeros((DEPTH, d, pad), F32)], axis=-1).astype(BF16)
    b_route = jnp.concatenate([b_group, b_router, jnp.zeros((DEPTH, pad), F32)], axis=-1)
    b_route = b_route.reshape(DEPTH, 1, ROUTE_W).astype(F32)
    ln_g3 = ln_g.reshape(DEPTH * 3, 1, d)
    ln_b3 = ln_b.reshape(DEPTH * 3, 1, d)

    mem_n = _ln(mem.reshape(-1, d), mem_ln_g, mem_ln_b, ROW_TILE)
    kv_all = _matmul(mem_n, w_kv, BF16, ROW_TILE, 2 * d)

    xt = x.reshape(n, d)
    for l in range(DEPTH):
        proj = _matmul(xt, w_in_r[l], F32, ROW_TILE // 2, PROJ_W)
        y_hg = _hgrn(proj, lb_tab[l], hg_norm_w[l], bsz, seq)
        y_gd = _gdn(proj, gd_conv_w[l], gd_a_log[l], gd_dt_bias[l], gd_norm_w[l], bsz, seq)
        x1 = _outproj_ln(y_hg, y_gd, w_out_b, l, xt, ln_g3, ln_b3)
        x2, route = _attn_router(x1, kv_all, w_mq_b, w_mo_b, w_route, b_route, l, ln_g3, ln_b3,
                                 bsz, seq)
        order, bstart, bend, block_expert, n_used = _routing_tables(route, n)
        y = _experts(x2, order, bstart, bend, block_expert, n_used, w_gate, w_up, w_down, l)
        xt = _combine_ln(y, x2, route, l, ln_g3, ln_b3)
    return xt.reshape(bsz, seq, d)
```

```python
import functools
import math

import jax
import jax.numpy as jnp
import numpy as np
from jax import lax
from jax.experimental import pallas as pl
from jax.experimental.pallas import tpu as pltpu

D_MODEL = 1024
DEPTH = 4
HG_HEADS = 4
HG_DK = 128
GD_HEADS = 4
GD_DK = 128
HEAD_W = 128
CONV_K = 4
MEM_HEADS = 4
MEM_DH = D_MODEL // MEM_HEADS
N_GROUPS = 4
EXPERTS_PER_GROUP = 8
N_EXPERTS = N_GROUPS * EXPERTS_PER_GROUP
TOP_K = 2
D_EXPERT = 512
ALPHA = (2.0 * DEPTH) ** 0.25
LN_EPS = 1e-5
RMS_EPS = 1e-6
L2_EPS = 1e-6

COL_HQ, COL_HF, COL_HI, COL_HZ, COL_GQ, COL_GK, COL_GV, COL_GZ, COL_GAB = 0, 4, 8, 12, 16, 20, 24, 28, 32
PROJ_W = 33 * HEAD_W

TIME_BLOCK = 256
GD_CHUNK = 64
ROW_TILE = 512
EXPERT_ROWS = 256
ROUTE_W = 128
LANES = 128
ROW_CHUNKS = D_MODEL // LANES
VMEM_LIMIT_BYTES = 48 * 1024 * 1024

BF16 = jnp.bfloat16
F32 = jnp.float32
NT_DIMS = (((1,), (1,)), ((), ()))
TN_DIMS = (((0,), (0,)), ((), ()))


def _params(n_grid):
    return pltpu.CompilerParams(dimension_semantics=("arbitrary",) * n_grid,
                                vmem_limit_bytes=VMEM_LIMIT_BYTES)


def _mm(a, b):
    return jnp.dot(a.astype(BF16), b.astype(BF16), preferred_element_type=F32)


def _mm_nt(a, b):
    return lax.dot_general(a.astype(BF16), b.astype(BF16), NT_DIMS, preferred_element_type=F32)


def _mm_tn(a, b):
    return lax.dot_general(a.astype(BF16), b.astype(BF16), TN_DIMS, preferred_element_type=F32)


def _cumsum_rows(tri, x):
    p0 = x.astype(BF16)
    r0 = x - p0.astype(F32)
    p1 = r0.astype(BF16)
    p2 = (r0 - p1.astype(F32)).astype(BF16)
    dot = lambda p: jnp.dot(tri, p, preferred_element_type=F32)
    return dot(p0) + dot(p1) + dot(p2)


def _sigmoid(x):
    e = jnp.exp(-jnp.abs(x))
    return jnp.where(x >= 0, 1.0, e) / (1.0 + e)


def _softplus(x):
    return jnp.maximum(x, 0.0) + jnp.log1p(jnp.exp(-jnp.abs(x)))


def _load_token_tiles(ref, n):
    return jnp.concatenate([ref[pl.ds(j, n, stride=ROW_CHUNKS), :] for j in range(ROW_CHUNKS)],
                           axis=1)


def _store_token_tiles(ref, x):
    for j in range(ROW_CHUNKS):
        ref[pl.ds(j, x.shape[0], stride=ROW_CHUNKS), :] = x[:, j * LANES:(j + 1) * LANES]


def _layer_norm(z, g, b):
    mu = jnp.mean(z, axis=-1, keepdims=True)
    zc = z - mu
    var = jnp.mean(zc * zc, axis=-1, keepdims=True)
    return zc * lax.rsqrt(var + LN_EPS) * g + b


def _matmul_kernel(x_ref, w_ref, o_ref):
    o_ref[...] = _mm(x_ref[...], w_ref[...]).astype(o_ref.dtype)


def _matmul(x, w, out_dtype, tm, tn):
    m, k = x.shape
    n = w.shape[1]
    return pl.pallas_call(
        _matmul_kernel,
        grid=(m // tm, n // tn),
        in_specs=[pl.BlockSpec((tm, k), lambda i, j: (i, 0)),
                  pl.BlockSpec((k, tn), lambda i, j: (0, j))],
        out_specs=pl.BlockSpec((tm, tn), lambda i, j: (i, j)),
        out_shape=jax.ShapeDtypeStruct((m, n), out_dtype),
        compiler_params=_params(2),
        name="matmul",
    )(x, w)


def _ln_kernel(x_ref, g_ref, b_ref, o_ref):
    o_ref[...] = _layer_norm(x_ref[...], g_ref[...], b_ref[...])


def _ln(x, g, b, tm):
    n, d = x.shape
    return pl.pallas_call(
        _ln_kernel,
        grid=(n // tm,),
        in_specs=[pl.BlockSpec((tm, d), lambda i: (i, 0)),
                  pl.BlockSpec((1, d), lambda i: (0, 0)),
                  pl.BlockSpec((1, d), lambda i: (0, 0))],
        out_specs=pl.BlockSpec((tm, d), lambda i: (i, 0)),
        out_shape=jax.ShapeDtypeStruct((n, d), F32),
        compiler_params=_params(1),
        name="mem_ln",
    )(x, g.reshape(1, d), b.reshape(1, d))


def _ref_rows(cum, row, h):
    c, w = cum.shape
    if 2 * h <= 8:
        j = row & (2 * h - 1)
        m = cum
        for dl in range(-(h - 1), h + 1):
            if dl != 0:
                m = jnp.where(j - (h - 1) == dl, pltpu.roll(cum, dl % c, axis=0), m)
        return m
    pieces = [jnp.broadcast_to(cum[s + h - 1:s + h, :], (2 * h, w)) for s in range(0, c, 2 * h)]
    return jnp.concatenate(pieces, axis=0) if len(pieces) > 1 else pieces[0]


def _hgrn_kernel(q_ref, f_ref, i_ref, z_ref, lb_ref, nw_ref, tri_ref, msk_ref, o_ref, st_ref):
    @pl.when(pl.program_id(1) == 0)
    def _():
        st_ref[...] = jnp.zeros_like(st_ref)

    c = q_ref.shape[0]
    zf = f_ref[...]
    log_lb, log1m_lb, one_m_lb = lb_ref[0:1, :], lb_ref[1:2, :], lb_ref[2:3, :]
    e = jnp.exp(-jnp.abs(zf))
    log_sig = jnp.minimum(zf, 0.0) - jnp.log1p(e)
    b = log1m_lb + log_sig
    log_f = jnp.maximum(log_lb, b) + jnp.log1p(jnp.exp(-jnp.abs(log_lb - b)))
    k_all = one_m_lb * (jnp.where(zf >= 0, e, 1.0) / (1.0 + e))
    cum_all = _cumsum_rows(tri_ref[...], log_f)

    row = lax.broadcasted_iota(jnp.int32, (c, 1), 0)
    n_lvl = msk_ref.shape[0]
    states = [st_ref[hd] for hd in range(HG_HEADS)]
    ys = []
    for hd in range(HG_HEADS):
        sl = slice(hd * HEAD_W, (hd + 1) * HEAD_W)
        q, k, v, cum = q_ref[:, sl], k_all[:, sl], i_ref[:, sl], cum_all[:, sl]
        scores = msk_ref[n_lvl - 1] * jnp.sum(q * k, axis=-1, keepdims=True)
        for lvl in range(n_lvl):
            h = 1 << lvl
            x = jnp.exp(-jnp.abs(cum - _ref_rows(cum, row, h)))
            upper = (row & (2 * h - 1)) >= h
            sc = _mm_nt(jnp.where(upper, q * x, 0.0), jnp.where(upper, 0.0, k * x))
            scores = scores + (sc * msk_ref[lvl] if 2 * h < c else sc)

        st = states[hd]
        o = _mm_nt(q * jnp.exp(cum), st) + _mm(scores, v)
        cum_last = cum[c - 1:c, :]
        states[hd] = jnp.exp(cum_last) * st + _mm_tn(v, k * jnp.exp(cum_last - cum))
        o = o * lax.rsqrt(jnp.mean(o * o, axis=-1, keepdims=True) + RMS_EPS) * nw_ref[...]
        ys.append((o * _sigmoid(z_ref[:, sl])).astype(o_ref.dtype))
    o_ref[...] = jnp.concatenate(ys, axis=1)
    for hd in range(HG_HEADS):
        st_ref[hd] = states[hd]


def _hgrn(proj, lb_tab, norm_w, bsz, seq):
    nt = seq // TIME_BLOCK
    width = HG_HEADS * HEAD_W
    idx = np.arange(TIME_BLOCK)
    tri = jnp.asarray((idx[:, None] >= idx[None, :]).astype(np.float32), BF16)
    n_lvl = int(math.log2(TIME_BLOCK))
    masks = [(idx[:, None] >> (l + 1)) == (idx[None, :] >> (l + 1)) for l in range(n_lvl - 1)]
    masks = jnp.asarray(np.stack(masks + [idx[:, None] == idx[None, :]]).astype(np.float32))
    col = lambda c0: pl.BlockSpec((TIME_BLOCK, width), lambda b, t: (b * nt + t, c0 // HG_HEADS))
    return pl.pallas_call(
        _hgrn_kernel,
        grid=(bsz, nt),
        in_specs=[col(COL_HQ), col(COL_HF), col(COL_HI), col(COL_HZ),
                  pl.BlockSpec((3, width), lambda b, t: (0, 0)),
                  pl.BlockSpec((1, HEAD_W), lambda b, t: (0, 0)),
                  pl.BlockSpec((TIME_BLOCK, TIME_BLOCK), lambda b, t: (0, 0)),
                  pl.BlockSpec((n_lvl, TIME_BLOCK, TIME_BLOCK), lambda b, t: (0, 0, 0))],
        out_specs=pl.BlockSpec((TIME_BLOCK, width), lambda b, t: (b * nt + t, 0)),
        out_shape=jax.ShapeDtypeStruct((bsz * seq, width), BF16),
        scratch_shapes=[pltpu.VMEM((HG_HEADS, HEAD_W, HG_DK), F32)],
        compiler_params=_params(2),
        name="hgrn2",
    )(proj, proj, proj, proj, lb_tab, norm_w.reshape(1, HEAD_W), tri, masks)


def _conv_silu(x_ref, w_ref, buf_ref, first):
    tb = x_ref.shape[0]

    @pl.when(first)
    def _():
        buf_ref[0:8, :] = jnp.zeros((8, buf_ref.shape[1]), F32)

    @pl.when(jnp.logical_not(first))
    def _():
        buf_ref[0:8, :] = buf_ref[tb:tb + 8, :]

    buf_ref[8:tb + 8, :] = x_ref[...]
    y = w_ref[CONV_K - 1:CONV_K, :] * x_ref[...]
    for j in range(1, CONV_K):
        y = y + w_ref[CONV_K - 1 - j:CONV_K - j, :] * buf_ref[8 - j:8 - j + tb, :]
    return y * _sigmoid(y)


def _gdn_kernel(q_ref, k_ref, v_ref, z_ref, ab_ref, wq_ref, wk_ref, wv_ref, alog_ref, dtb_ref,
                nw_ref, tri_ref, msk_ref, o_ref, s_ref, bq_ref, bk_ref, bv_ref):
    first = pl.program_id(1) == 0

    @pl.when(first)
    def _():
        s_ref[...] = jnp.zeros_like(s_ref)

    tb = q_ref.shape[0]
    cq_all = _conv_silu(q_ref, wq_ref, bq_ref, first)
    ck_all = _conv_silu(k_ref, wk_ref, bk_ref, first)
    v_all = _conv_silu(v_ref, wv_ref, bv_ref, first)

    lane = lax.broadcasted_iota(jnp.int32, (1, HEAD_W), 1)
    ab = ab_ref[...]
    g_all = jnp.where(lane < GD_HEADS, -jnp.exp(alog_ref[...]) * _softplus(ab + dtb_ref[...]), 0.0)
    beta_all = _sigmoid(ab)
    lc_all = _cumsum_rows(tri_ref[...], g_all)
    lc_all_t = jnp.transpose(lc_all)
    n_sq = int(math.log2(GD_CHUNK)) - 1

    heads = range(GD_HEADS)
    sls = [slice(hd * HEAD_W, (hd + 1) * HEAD_W) for hd in heads]
    s = [s_ref[hd] for hd in heads]
    q, k, kb, beta, lc_col, qk, t, pw = [], [], [], [], [], [], [], []
    for hd in heads:
        cq, ck = cq_all[:, sls[hd]], ck_all[:, sls[hd]]
        q.append(cq * lax.rsqrt(jnp.sum(cq * cq, axis=-1, keepdims=True) + L2_EPS) * (GD_DK ** -0.5))
        k.append(ck * lax.rsqrt(jnp.sum(ck * ck, axis=-1, keepdims=True) + L2_EPS))
        beta.append(beta_all[:, GD_HEADS + hd:GD_HEADS + hd + 1])
        lc_col.append(lc_all[:, hd:hd + 1])
        decay = jnp.exp(lc_col[hd] - lc_all_t[hd:hd + 1, :] + msk_ref[0])
        kb.append(k[hd] * beta[hd])
        p = _mm_nt(jnp.concatenate([q[hd], kb[hd]], axis=0), k[hd])
        qk.append(p[:tb] * decay)
        a = p[tb:] * decay * msk_ref[1]
        t.append(msk_ref[2] - a)
        pw.append(a)
    for _ in range(n_sq):
        for hd in heads:
            pw[hd] = _mm(pw[hd], pw[hd])
            t[hd] = t[hd] + _mm(t[hd], pw[hd])
    u, w, qe = [], [], []
    for hd in heads:
        e_lc = jnp.exp(lc_col[hd])
        uw = _mm(t[hd], jnp.concatenate([v_all[:, sls[hd]] * beta[hd], kb[hd] * e_lc], axis=1))
        u.append(uw[:, :HEAD_W])
        w.append(uw[:, HEAD_W:])
        qe.append(q[hd] * e_lc)
    outs = [[] for _ in heads]
    for c0 in range(0, tb, GD_CHUNK):
        c1 = c0 + GD_CHUNK
        for hd in heads:
            r = _mm(jnp.concatenate([w[hd][c0:c1], qe[hd][c0:c1]], axis=0), s[hd])
            v_new = u[hd][c0:c1] - r[:GD_CHUNK]
            pads = ([jnp.zeros((c0, HEAD_W), F32)] if c0 else []) + [v_new] + (
                [jnp.zeros((tb - c1, HEAD_W), F32)] if c1 < tb else [])
            outs[hd].append(r[GD_CHUNK:] + _mm(qk[hd][c0:c1, :], jnp.concatenate(pads, axis=0)))
            lc_last = lc_col[hd][c1 - 1:c1, :]
            kd = k[hd][c0:c1] * jnp.exp(lc_last - lc_col[hd][c0:c1])
            s[hd] = jnp.exp(lc_last) * s[hd] + _mm_tn(kd, v_new)
    ys = []
    for hd in heads:
        o = jnp.concatenate(outs[hd], axis=0)
        o = o * lax.rsqrt(jnp.mean(o * o, axis=-1, keepdims=True) + RMS_EPS) * nw_ref[...]
        zg = z_ref[:, sls[hd]]
        ys.append((o * (zg * _sigmoid(zg))).astype(o_ref.dtype))
    o_ref[...] = jnp.concatenate(ys, axis=1)
    for hd in heads:
        s_ref[hd] = s[hd]


def _gdn(proj, conv_w, a_log, dt_bias, norm_w, bsz, seq):
    nt = seq // TIME_BLOCK
    idx = np.arange(TIME_BLOCK)
    tri = (idx[:, None] >= idx[None, :]) & (idx[:, None] // GD_CHUNK == idx[None, :] // GD_CHUNK)
    eye = idx[:, None] == idx[None, :]
    masks = jnp.asarray(np.stack([np.where(tri, 0.0, -np.inf), 1.0 - eye, eye]).astype(np.float32))
    tri = jnp.asarray(tri.astype(np.float32), BF16)
    width = GD_HEADS * HEAD_W
    lane_pad = lambda p: jnp.concatenate([p.astype(F32), jnp.zeros((HEAD_W - GD_HEADS,), F32)]
                                         ).reshape(1, HEAD_W)
    col = lambda c0: pl.BlockSpec((TIME_BLOCK, width), lambda b, t: (b * nt + t, c0 // GD_HEADS))
    cw = lambda c0: pl.BlockSpec((CONV_K, width), lambda b, t: (0, c0))
    row = lambda: pl.BlockSpec((1, HEAD_W), lambda b, t: (0, 0))
    return pl.pallas_call(
        _gdn_kernel,
        grid=(bsz, nt),
        in_specs=[col(COL_GQ), col(COL_GK), col(COL_GV), col(COL_GZ),
                  pl.BlockSpec((TIME_BLOCK, HEAD_W), lambda b, t: (b * nt + t, COL_GAB)),
                  cw(0), cw(1), cw(2), row(), row(), row(),
                  pl.BlockSpec((TIME_BLOCK, TIME_BLOCK), lambda b, t: (0, 0)),
                  pl.BlockSpec((3, TIME_BLOCK, TIME_BLOCK), lambda b, t: (0, 0, 0))],
        out_specs=pl.BlockSpec((TIME_BLOCK, width), lambda b, t: (b * nt + t, 0)),
        out_shape=jax.ShapeDtypeStruct((bsz * seq, width), BF16),
        scratch_shapes=[pltpu.VMEM((GD_HEADS, GD_DK, HEAD_W), F32)] + [
            pltpu.VMEM((TIME_BLOCK + 8, width), F32) for _ in range(3)],
        compiler_params=_params(2),
        name="gated_deltanet",
    )(proj, proj, proj, proj, proj, conv_w, conv_w, conv_w, lane_pad(a_log), lane_pad(dt_bias),
      norm_w.reshape(1, HEAD_W), tri, masks)


def _outproj_ln_kernel(yh_ref, yg_ref, wh_ref, wg_ref, x_ref, g_ref, b_ref, o_ref):
    mix = (jnp.dot(yh_ref[...], wh_ref[...], preferred_element_type=F32)
           + jnp.dot(yg_ref[...], wg_ref[...], preferred_element_type=F32))
    o_ref[...] = _layer_norm(ALPHA * x_ref[...] + mix, g_ref[...], b_ref[...])


def _outproj_ln(y_hg, y_gd, w_out, layer, x, ln_g, ln_b):
    n, d = x.shape
    hw = y_hg.shape[1]
    tm = ROW_TILE
    return pl.pallas_call(
        _outproj_ln_kernel,
        grid=(n // tm,),
        in_specs=[pl.BlockSpec((tm, hw), lambda i: (i, 0)),
                  pl.BlockSpec((tm, hw), lambda i: (i, 0)),
                  pl.BlockSpec((None, hw, d), lambda i: (layer, 0, 0)),
                  pl.BlockSpec((None, hw, d), lambda i: (layer, 1, 0)),
                  pl.BlockSpec((tm, d), lambda i: (i, 0)),
                  pl.BlockSpec((None, 1, d), lambda i: (layer * 3, 0, 0)),
                  pl.BlockSpec((None, 1, d), lambda i: (layer * 3, 0, 0))],
        out_specs=pl.BlockSpec((tm, d), lambda i: (i, 0)),
        out_shape=jax.ShapeDtypeStruct((n, d), F32),
        compiler_params=_params(1),
        name="outproj_ln",
    )(y_hg, y_gd, w_out, w_out, x, ln_g, ln_b)


def _attn_router_kernel(x_ref, wq_ref, k_ref, v_ref, wo_ref, g_ref, b_ref, wr_ref, br_ref,
                        o_ref, route_ref):
    x = x_ref[...]
    q = _mm(x, wq_ref[...]).astype(BF16)
    heads = []
    for h in range(MEM_HEADS):
        sl = slice(h * MEM_DH, (h + 1) * MEM_DH)
        s = lax.dot_general(q[:, sl], k_ref[:, sl], NT_DIMS, preferred_element_type=F32)
        s = s * (MEM_DH ** -0.5)
        p = jnp.exp(s - jnp.max(s, axis=-1, keepdims=True))
        p = p / jnp.sum(p, axis=-1, keepdims=True)
        heads.append(_mm(p, v_ref[:, sl]))
    xa = _mm(jnp.concatenate(heads, axis=1), wo_ref[...])
    x2 = _layer_norm(ALPHA * x + xa, g_ref[...], b_ref[...])
    _store_token_tiles(o_ref, x2)

    logits = _mm(x2, wr_ref[...]) + br_ref[...]
    lane = lax.broadcasted_iota(jnp.int32, (1, ROUTE_W), 1).astype(F32)
    neg = -jnp.inf
    first_at = lambda val, top: jnp.min(jnp.where(val == top, lane, float(ROUTE_W)), axis=-1,
                                        keepdims=True)
    gl = jnp.where(lane < N_GROUPS, logits, neg)
    g_max = jnp.max(gl, axis=-1, keepdims=True)
    g_sel = first_at(gl, g_max)
    g_gate = 1.0 / jnp.sum(jnp.exp(gl - g_max), axis=-1, keepdims=True)
    lo = N_GROUPS + EXPERTS_PER_GROUP * g_sel
    el = jnp.where((lane >= lo) & (lane < lo + EXPERTS_PER_GROUP), logits, neg)
    t1 = jnp.max(el, axis=-1, keepdims=True)
    i1 = first_at(el, t1)
    el2 = jnp.where(lane == i1, neg, el)
    t2 = jnp.max(el2, axis=-1, keepdims=True)
    i2 = first_at(el2, t2)
    e2 = jnp.exp(t2 - t1)
    w1 = g_gate / (1.0 + e2)
    w2 = g_gate * e2 / (1.0 + e2)
    route_ref[...] = jnp.where(lane == 0, i1 - N_GROUPS,
                               jnp.where(lane == 1, i2 - N_GROUPS,
                                         jnp.where(lane == 2, w1, jnp.where(lane == 3, w2, 0.0))))


def _attn_router(x, kv_all, w_mq, w_mo, w_route, b_route, layer, ln_g, ln_b, bsz, seq):
    n, d = x.shape
    tq = ROW_TILE
    nq = seq // tq
    mlen = kv_all.shape[0] // bsz
    wspec = lambda: pl.BlockSpec((None, d, d), lambda b, i: (layer, 0, 0))
    lnspec = lambda: pl.BlockSpec((None, 1, d), lambda b, i: (layer * 3 + 1, 0, 0))
    return pl.pallas_call(
        _attn_router_kernel,
        grid=(bsz, nq),
        in_specs=[pl.BlockSpec((tq, d), lambda b, i: (b * nq + i, 0)),
                  wspec(),
                  pl.BlockSpec((mlen, d), lambda b, i: (b, 2 * layer)),
                  pl.BlockSpec((mlen, d), lambda b, i: (b, 2 * layer + 1)),
                  wspec(), lnspec(), lnspec(),
                  pl.BlockSpec((None, d, ROUTE_W), lambda b, i: (layer, 0, 0)),
                  pl.BlockSpec((None, 1, ROUTE_W), lambda b, i: (layer, 0, 0))],
        out_specs=[pl.BlockSpec((tq * ROW_CHUNKS, LANES), lambda b, i: (b * nq + i, 0)),
                   pl.BlockSpec((tq, ROUTE_W), lambda b, i: (b * nq + i, 0))],
        out_shape=[jax.ShapeDtypeStruct((n * ROW_CHUNKS, LANES), F32),
                   jax.ShapeDtypeStruct((n, ROUTE_W), F32)],
        compiler_params=_params(2),
        name="attn_router",
    )(x, w_mq, kv_all, kv_all, w_mo, ln_g, ln_b, w_route, b_route)


def _expert_kernel(order_ref, bstart_ref, bend_ref, be_ref, nu_ref, x_hbm, zeros_hbm, wg_ref, wu_ref,
                   wd_ref, y_hbm, rows_ref, yv_ref, gsem, ssem, zsem, wg_bf, wu_bf, wd_bf):
    b = pl.program_id(0)
    nu = nu_ref[0]
    m = order_ref.shape[0] - EXPERT_ROWS
    n_tok = m // TOP_K
    slot = b % 2

    k_bits = int(math.log2(TOP_K))

    def tile(r):
        return pl.ds(pl.multiple_of(r * ROW_CHUNKS, ROW_CHUNKS), ROW_CHUNKS)

    def gather(blk, s):
        base = bstart_ref[blk]

        def body(i, carry):
            a = order_ref[base + i]
            pltpu.make_async_copy(x_hbm.at[tile(a >> k_bits), :], rows_ref.at[s, tile(i), :],
                                  gsem.at[s]).start()
            return carry
        lax.fori_loop(0, EXPERT_ROWS, body, 0, unroll=8)

    def scatter(blk, s):
        base = bstart_ref[blk]
        n_valid = bend_ref[blk] - base

        def body(i, carry):
            a = order_ref[base + i]
            row = jnp.where(i < n_valid, (a & (TOP_K - 1)) * n_tok + (a >> k_bits), m + i)
            pltpu.make_async_copy(yv_ref.at[s, tile(i), :], y_hbm.at[tile(row), :],
                                  ssem.at[s]).start()
            return carry
        lax.fori_loop(0, EXPERT_ROWS, body, 0, unroll=8)

    block_rows = EXPERT_ROWS * ROW_CHUNKS

    def wait_gather(s):
        pltpu.make_async_copy(x_hbm.at[pl.ds(0, block_rows), :], rows_ref.at[s], gsem.at[s]).wait()

    def wait_scatter(s):
        pltpu.make_async_copy(yv_ref.at[s], y_hbm.at[pl.ds(0, block_rows), :], ssem.at[s]).wait()

    @pl.when(b == 0)
    def _():
        zc = pltpu.make_async_copy(zeros_hbm, y_hbm.at[pl.ds(m * ROW_CHUNKS, block_rows), :], zsem)
        zc.start()
        zc.wait()
        gather(0, 0)

    @pl.when(b + 1 < nu)
    def _():
        gather(b + 1, 1 - slot)

    @pl.when(jnp.logical_or(b == 0, be_ref[b] != be_ref[jnp.maximum(b - 1, 0)]))
    def _():
        wg_bf[...] = wg_ref[...].astype(BF16)
        wu_bf[...] = wu_ref[...].astype(BF16)
        wd_bf[...] = wd_ref[...].astype(BF16)

    @pl.when(b < nu)
    def _():
        wait_gather(slot)
        xb = _load_token_tiles(rows_ref.at[slot], EXPERT_ROWS).astype(BF16)
        hg = jnp.dot(xb, wg_bf[...], preferred_element_type=F32)
        hu = jnp.dot(xb, wu_bf[...], preferred_element_type=F32)
        hid = hg * _sigmoid(hg) * hu
        y = jnp.dot(hid.astype(BF16), wd_bf[...], preferred_element_type=F32)

        @pl.when(b >= 2)
        def _():
            wait_scatter(slot)

        _store_token_tiles(yv_ref.at[slot], y)
        scatter(b, slot)

        @pl.when(b == nu - 1)
        def _():
            wait_scatter(slot)

            @pl.when(b >= 1)
            def _():
                wait_scatter(1 - slot)


def _experts(x, order, bstart, bend, block_expert, n_used, w_gate, w_up, w_down, layer):
    d = D_MODEL
    m = order.shape[0] - EXPERT_ROWS
    block_rows = EXPERT_ROWS * ROW_CHUNKS
    nb = m // EXPERT_ROWS + N_EXPERTS
    any_spec = pl.BlockSpec(memory_space=pl.ANY)
    wspec = lambda r, c: pl.BlockSpec((None, None, r, c),
                                      lambda b, o, bs, bn, be, nu: (layer, be[b], 0, 0))
    grid_spec = pltpu.PrefetchScalarGridSpec(
        num_scalar_prefetch=5, grid=(nb,),
        in_specs=[any_spec, any_spec, wspec(d, D_EXPERT), wspec(d, D_EXPERT), wspec(D_EXPERT, d)],
        out_specs=any_spec,
        scratch_shapes=[pltpu.VMEM((2, block_rows, LANES), F32), pltpu.VMEM((2, block_rows, LANES), F32),
                        pltpu.SemaphoreType.DMA((2,)), pltpu.SemaphoreType.DMA((2,)),
                        pltpu.SemaphoreType.DMA(()),
                        pltpu.VMEM((d, D_EXPERT), BF16), pltpu.VMEM((d, D_EXPERT), BF16),
                        pltpu.VMEM((D_EXPERT, d), BF16)])
    return pl.pallas_call(
        _expert_kernel,
        grid_spec=grid_spec,
        out_shape=jax.ShapeDtypeStruct((m * ROW_CHUNKS + block_rows, LANES), F32),
        compiler_params=_params(1),
        name="moe_experts",
    )(order, bstart, bend, block_expert, n_used, x, jnp.zeros((block_rows, LANES), F32),
      w_gate, w_up, w_down)


def _combine_ln_kernel(y0_ref, y1_ref, x_ref, route_ref, g_ref, b_ref, o_ref):
    route = route_ref[...]
    tm = route.shape[0]
    ff = (route[:, 2:3] * _load_token_tiles(y0_ref, tm)
          + route[:, 3:4] * _load_token_tiles(y1_ref, tm))
    o_ref[...] = _layer_norm(ALPHA * _load_token_tiles(x_ref, tm) + ff, g_ref[...], b_ref[...])


def _combine_ln(y, x, route, layer, ln_g, ln_b):
    n, d = route.shape[0], D_MODEL
    tm = ROW_TILE
    lnspec = lambda: pl.BlockSpec((None, 1, d), lambda i: (layer * 3 + 2, 0, 0))
    tiles = lambda off: pl.BlockSpec((tm * ROW_CHUNKS, LANES), lambda i: (off + i, 0))
    return pl.pallas_call(
        _combine_ln_kernel,
        grid=(n // tm,),
        in_specs=[tiles(0), tiles(n // tm), tiles(0),
                  pl.BlockSpec((tm, ROUTE_W), lambda i: (i, 0)),
                  lnspec(), lnspec()],
        out_specs=pl.BlockSpec((tm, d), lambda i: (i, 0)),
        out_shape=jax.ShapeDtypeStruct((n, d), F32),
        compiler_params=_params(1),
        name="moe_combine_ln",
    )(y, y, x, route, ln_g, ln_b)


def _routing_tables(route, n_tok):
    m = n_tok * TOP_K
    flat_e = route[:, 0:TOP_K].astype(jnp.int32).reshape(m)
    order = jnp.concatenate([jnp.argsort(flat_e).astype(jnp.int32),
                             jnp.zeros((EXPERT_ROWS,), jnp.int32)])
    experts = jnp.arange(N_EXPERTS, dtype=jnp.int32)
    counts = jnp.sum((flat_e[:, None] == experts[None, :]).astype(jnp.int32), axis=0)
    starts = jnp.cumsum(counts) - counts
    pcounts = (counts + EXPERT_ROWS - 1) // EXPERT_ROWS * EXPERT_ROWS
    pends = jnp.cumsum(pcounts)
    n_blocks = m // EXPERT_ROWS + N_EXPERTS
    n_used = pends[-1] // EXPERT_ROWS
    blk = jnp.minimum(jnp.arange(n_blocks, dtype=jnp.int32), n_used - 1)
    block_expert = jnp.minimum(jnp.searchsorted(pends, blk * EXPERT_ROWS, side='right'),
                               N_EXPERTS - 1).astype(jnp.int32)
    bstart = starts[block_expert] + blk * EXPERT_ROWS - (pends - pcounts)[block_expert]
    bend = (starts + counts)[block_expert]
    return (order, bstart.astype(jnp.int32), bend.astype(jnp.int32), block_expert,
            n_used.reshape(1).astype(jnp.int32))


def kernel(x, mem, w_in, hg_lb_logits, hg_norm_w, gd_conv_w, gd_a_log, gd_dt_bias, gd_norm_w, w_out,
           mem_ln_g, mem_ln_b, w_mq, w_mk, w_mv, w_mo, w_group, b_group, w_router, b_router,
           w_gate, w_up, w_down, ln_g, ln_b):
    bsz, seq, d = x.shape
    n = bsz * seq
    lb_all = jnp.cumsum(jax.nn.softmax(hg_lb_logits.astype(F32), axis=0), axis=0)
    lb_all = lb_all - lb_all[0]
    lb_tab = jnp.stack([jnp.log(lb_all), jnp.log1p(-lb_all), 1.0 - lb_all], axis=1)
    sp = np.cumsum((512, 512, 512, 512, 512, 512, 512, GD_HEADS, GD_HEADS, 512))
    seg = lambda a, b: w_in[:, :, (sp[a - 1] if a else 0):sp[b]]
    w_in_r = jnp.concatenate(
        [seg(0, 6), seg(9, 9), seg(7, 8), jnp.zeros((DEPTH, d, HEAD_W - 2 * GD_HEADS), F32)],
        axis=-1).astype(BF16)
    w_out_b = w_out.astype(BF16)
    w_mq_b, w_mo_b = w_mq.astype(BF16), w_mo.astype(BF16)
    w_kv = jnp.stack([w_mk, w_mv], axis=1).astype(BF16)
    w_kv = w_kv.transpose(2, 0, 1, 3).reshape(d, DEPTH * 2 * d)
    pad = ROUTE_W - N_GROUPS - N_EXPERTS
    w_route = jnp.concatenate([w_group, w_router, jnp.zeros((DEPTH, d, pad), F32)], axis=-1).astype(BF16)
    b_route = jnp.concatenate([b_group, b_router, jnp.zeros((DEPTH, pad), F32)], axis=-1)
    b_route = b_route.reshape(DEPTH, 1, ROUTE_W).astype(F32)
    ln_g3 = ln_g.reshape(DEPTH * 3, 1, d)
    ln_b3 = ln_b.reshape(DEPTH * 3, 1, d)

    mem_n = _ln(mem.reshape(-1, d), mem_ln_g, mem_ln_b, ROW_TILE)
    kv_all = _matmul(mem_n, w_kv, BF16, ROW_TILE, 2 * d)

    xt = x.reshape(n, d)
    for l in range(DEPTH):
        proj = _matmul(xt, w_in_r[l], F32, ROW_TILE // 2, PROJ_W)
        y_hg = _hgrn(proj, lb_tab[l], hg_norm_w[l], bsz, seq)
        y_gd = _gdn(proj, gd_conv_w[l], gd_a_log[l], gd_dt_bias[l], gd_norm_w[l], bsz, seq)
        x1 = _outproj_ln(y_hg, y_gd, w_out_b, l, xt, ln_g3, ln_b3)
        x2, route = _attn_router(x1, kv_all, w_mq_b, w_mo_b, w_route, b_route, l, ln_g3, ln_b3,
                                 bsz, seq)
        order, bstart, bend, block_expert, n_used = _routing_tables(route, n)
        y = _experts(x2, order, bstart, bend, block_expert, n_used, w_gate, w_up, w_down, l)
        xt = _combine_ln(y, x2, route, l, ln_g3, ln_b3)
    return xt.reshape(bsz, seq, d)
```

```python
import functools
import math

import jax
import jax.numpy as jnp
import numpy as np
from jax import lax
from jax.experimental import pallas as pl
from jax.experimental.pallas import tpu as pltpu

D_MODEL = 1024
DEPTH = 4
HG_HEADS = 4
HG_DK = 128
GD_HEADS = 4
GD_DK = 128
HEAD_W = 128
CONV_K = 4
MEM_HEADS = 4
MEM_DH = D_MODEL // MEM_HEADS
N_GROUPS = 4
EXPERTS_PER_GROUP = 8
N_EXPERTS = N_GROUPS * EXPERTS_PER_GROUP
TOP_K = 2
D_EXPERT = 512
ALPHA = (2.0 * DEPTH) ** 0.25
LN_EPS = 1e-5
RMS_EPS = 1e-6
L2_EPS = 1e-6

COL_HQ, COL_HF, COL_HI, COL_HZ, COL_GQ, COL_GK, COL_GV, COL_GZ, COL_GAB = 0, 4, 8, 12, 16, 20, 24, 28, 32
PROJ_W = 33 * HEAD_W

TIME_BLOCK = 256
GD_CHUNK = 64
ROW_TILE = 512
EXPERT_ROWS = 256
ROUTE_W = 128
LANES = 128
ROW_CHUNKS = D_MODEL // LANES
VMEM_LIMIT_BYTES = 48 * 1024 * 1024

BF16 = jnp.bfloat16
F32 = jnp.float32
NT_DIMS = (((1,), (1,)), ((), ()))
TN_DIMS = (((0,), (0,)), ((), ()))


def _params(n_grid):
    return pltpu.CompilerParams(dimension_semantics=("arbitrary",) * n_grid,
                                vmem_limit_bytes=VMEM_LIMIT_BYTES)


def _mm(a, b):
    return jnp.dot(a.astype(BF16), b.astype(BF16), preferred_element_type=F32)


def _mm_nt(a, b):
    return lax.dot_general(a.astype(BF16), b.astype(BF16), NT_DIMS, preferred_element_type=F32)


def _mm_tn(a, b):
    return lax.dot_general(a.astype(BF16), b.astype(BF16), TN_DIMS, preferred_element_type=F32)


def _cumsum_rows(tri, x):
    p0 = x.astype(BF16)
    r0 = x - p0.astype(F32)
    p1 = r0.astype(BF16)
    p2 = (r0 - p1.astype(F32)).astype(BF16)
    dot = lambda p: jnp.dot(tri, p, preferred_element_type=F32)
    return dot(p0) + dot(p1) + dot(p2)


def _sigmoid(x):
    e = jnp.exp(-jnp.abs(x))
    return jnp.where(x >= 0, 1.0, e) / (1.0 + e)


def _softplus(x):
    return jnp.maximum(x, 0.0) + jnp.log1p(jnp.exp(-jnp.abs(x)))


def _load_token_tiles(ref, n):
    return jnp.concatenate([ref[pl.ds(j, n, stride=ROW_CHUNKS), :] for j in range(ROW_CHUNKS)],
                           axis=1)


def _store_token_tiles(ref, x):
    for j in range(ROW_CHUNKS):
        ref[pl.ds(j, x.shape[0], stride=ROW_CHUNKS), :] = x[:, j * LANES:(j + 1) * LANES]


def _layer_norm(z, g, b):
    mu = jnp.mean(z, axis=-1, keepdims=True)
    zc = z - mu
    var = jnp.mean(zc * zc, axis=-1, keepdims=True)
    return zc * lax.rsqrt(var + LN_EPS) * g + b


def _matmul_kernel(x_ref, w_ref, o_ref):
    o_ref[...] = _mm(x_ref[...], w_ref[...]).astype(o_ref.dtype)


def _matmul(x, w, out_dtype, tm, tn):
    m, k = x.shape
    n = w.shape[1]
    return pl.pallas_call(
        _matmul_kernel,
        grid=(m // tm, n // tn),
        in_specs=[pl.BlockSpec((tm, k), lambda i, j: (i, 0)),
                  pl.BlockSpec((k, tn), lambda i, j: (0, j))],
        out_specs=pl.BlockSpec((tm, tn), lambda i, j: (i, j)),
        out_shape=jax.ShapeDtypeStruct((m, n), out_dtype),
        compiler_params=_params(2),
        name="matmul",
    )(x, w)


def _ln_kernel(x_ref, g_ref, b_ref, o_ref):
    o_ref[...] = _layer_norm(x_ref[...], g_ref[...], b_ref[...])


def _ln(x, g, b, tm):
    n, d = x.shape
    return pl.pallas_call(
        _ln_kernel,
        grid=(n // tm,),
        in_specs=[pl.BlockSpec((tm, d), lambda i: (i, 0)),
                  pl.BlockSpec((1, d), lambda i: (0, 0)),
                  pl.BlockSpec((1, d), lambda i: (0, 0))],
        out_specs=pl.BlockSpec((tm, d), lambda i: (i, 0)),
        out_shape=jax.ShapeDtypeStruct((n, d), F32),
        compiler_params=_params(1),
        name="mem_ln",
    )(x, g.reshape(1, d), b.reshape(1, d))


def _ref_rows(cum, row, h):
    c, w = cum.shape
    if 2 * h <= 8:
        j = row & (2 * h - 1)
        m = cum
        for dl in range(-(h - 1), h + 1):
            if dl != 0:
                m = jnp.where(j - (h - 1) == dl, pltpu.roll(cum, dl % c, axis=0), m)
        return m
    pieces = [jnp.broadcast_to(cum[s + h - 1:s + h, :], (2 * h, w)) for s in range(0, c, 2 * h)]
    return jnp.concatenate(pieces, axis=0) if len(pieces) > 1 else pieces[0]


def _hgrn_kernel(q_ref, f_ref, i_ref, z_ref, lb_ref, nw_ref, tri_ref, msk_ref, o_ref, st_ref):
    @pl.when(pl.program_id(1) == 0)
    def _():
        st_ref[...] = jnp.zeros_like(st_ref)

    c = q_ref.shape[0]
    zf = f_ref[...]
    log_lb, log1m_lb, one_m_lb = lb_ref[0:1, :], lb_ref[1:2, :], lb_ref[2:3, :]
    e = jnp.exp(-jnp.abs(zf))
    log_sig = jnp.minimum(zf, 0.0) - jnp.log1p(e)
    b = log1m_lb + log_sig
    log_f = jnp.maximum(log_lb, b) + jnp.log1p(jnp.exp(-jnp.abs(log_lb - b)))
    k_all = one_m_lb * (jnp.where(zf >= 0, e, 1.0) / (1.0 + e))
    cum_all = _cumsum_rows(tri_ref[...], log_f)

    row = lax.broadcasted_iota(jnp.int32, (c, 1), 0)
    n_lvl = msk_ref.shape[0]
    states = [st_ref[hd] for hd in range(HG_HEADS)]
    ys = []
    for hd in range(HG_HEADS):
        sl = slice(hd * HEAD_W, (hd + 1) * HEAD_W)
        q, k, v, cum = q_ref[:, sl], k_all[:, sl], i_ref[:, sl], cum_all[:, sl]
        scores = msk_ref[n_lvl - 1] * jnp.sum(q * k, axis=-1, keepdims=True)
        for lvl in range(n_lvl):
            h = 1 << lvl
            x = jnp.exp(-jnp.abs(cum - _ref_rows(cum, row, h)))
            upper = (row & (2 * h - 1)) >= h
            sc = _mm_nt(jnp.where(upper, q * x, 0.0), jnp.where(upper, 0.0, k * x))
            scores = scores + (sc * msk_ref[lvl] if 2 * h < c else sc)

        st = states[hd]
        o = _mm_nt(q * jnp.exp(cum), st) + _mm(scores, v)
        cum_last = cum[c - 1:c, :]
        states[hd] = jnp.exp(cum_last) * st + _mm_tn(v, k * jnp.exp(cum_last - cum))
        o = o * lax.rsqrt(jnp.mean(o * o, axis=-1, keepdims=True) + RMS_EPS) * nw_ref[...]
        ys.append((o * _sigmoid(z_ref[:, sl])).astype(o_ref.dtype))
    o_ref[...] = jnp.concatenate(ys, axis=1)
    for hd in range(HG_HEADS):
        st_ref[hd] = states[hd]


def _hgrn(proj, lb_tab, norm_w, bsz, seq):
    nt = seq // TIME_BLOCK
    width = HG_HEADS * HEAD_W
    idx = np.arange(TIME_BLOCK)
    tri = jnp.asarray((idx[:, None] >= idx[None, :]).astype(np.float32), BF16)
    n_lvl = int(math.log2(TIME_BLOCK))
    masks = [(idx[:, None] >> (l + 1)) == (idx[None, :] >> (l + 1)) for l in range(n_lvl - 1)]
    masks = jnp.asarray(np.stack(masks + [idx[:, None] == idx[None, :]]).astype(np.float32))
    col = lambda c0: pl.BlockSpec((TIME_BLOCK, width), lambda b, t: (b * nt + t, c0 // HG_HEADS))
    return pl.pallas_call(
        _hgrn_kernel,
        grid=(bsz, nt),
        in_specs=[col(COL_HQ), col(COL_HF), col(COL_HI), col(COL_HZ),
                  pl.BlockSpec((3, width), lambda b, t: (0, 0)),
                  pl.BlockSpec((1, HEAD_W), lambda b, t: (0, 0)),
                  pl.BlockSpec((TIME_BLOCK, TIME_BLOCK), lambda b, t: (0, 0)),
                  pl.BlockSpec((n_lvl, TIME_BLOCK, TIME_BLOCK), lambda b, t: (0, 0, 0))],
        out_specs=pl.BlockSpec((TIME_BLOCK, width), lambda b, t: (b * nt + t, 0)),
        out_shape=jax.ShapeDtypeStruct((bsz * seq, width), BF16),
        scratch_shapes=[pltpu.VMEM((HG_HEADS, HEAD_W, HG_DK), F32)],
        compiler_params=_params(2),
        name="hgrn2",
    )(proj, proj, proj, proj, lb_tab, norm_w.reshape(1, HEAD_W), tri, masks)


def _conv_silu(x_ref, w_ref, buf_ref, first):
    tb = x_ref.shape[0]

    @pl.when(first)
    def _():
        buf_ref[0:8, :] = jnp.zeros((8, buf_ref.shape[1]), F32)

    @pl.when(jnp.logical_not(first))
    def _():
        buf_ref[0:8, :] = buf_ref[tb:tb + 8, :]

    buf_ref[8:tb + 8, :] = x_ref[...]
    y = w_ref[CONV_K - 1:CONV_K, :] * x_ref[...]
    for j in range(1, CONV_K):
        y = y + w_ref[CONV_K - 1 - j:CONV_K - j, :] * buf_ref[8 - j:8 - j + tb, :]
    return y * _sigmoid(y)


def _gdn_kernel(q_ref, k_ref, v_ref, z_ref, ab_ref, wq_ref, wk_ref, wv_ref, alog_ref, dtb_ref,
                nw_ref, tri_ref, msk_ref, o_ref, s_ref, bq_ref, bk_ref, bv_ref):
    first = pl.program_id(1) == 0

    @pl.when(first)
    def _():
        s_ref[...] = jnp.zeros_like(s_ref)

    tb = q_ref.shape[0]
    cq_all = _conv_silu(q_ref, wq_ref, bq_ref, first)
    ck_all = _conv_silu(k_ref, wk_ref, bk_ref, first)
    v_all = _conv_silu(v_ref, wv_ref, bv_ref, first)

    lane = lax.broadcasted_iota(jnp.int32, (1, HEAD_W), 1)
    ab = ab_ref[...]
    g_all = jnp.where(lane < GD_HEADS, -jnp.exp(alog_ref[...]) * _softplus(ab + dtb_ref[...]), 0.0)
    beta_all = _sigmoid(ab)
    lc_all = _cumsum_rows(tri_ref[...], g_all)
    lc_all_t = jnp.transpose(lc_all)
    n_sq = int(math.log2(GD_CHUNK)) - 1

    heads = range(GD_HEADS)
    sls = [slice(hd * HEAD_W, (hd + 1) * HEAD_W) for hd in heads]
    s = [s_ref[hd] for hd in heads]
    q, k, kb, beta, lc_col, qk, t, pw = [], [], [], [], [], [], [], []
    for hd in heads:
        cq, ck = cq_all[:, sls[hd]], ck_all[:, sls[hd]]
        q.append(cq * lax.rsqrt(jnp.sum(cq * cq, axis=-1, keepdims=True) + L2_EPS) * (GD_DK ** -0.5))
        k.append(ck * lax.rsqrt(jnp.sum(ck * ck, axis=-1, keepdims=True) + L2_EPS))
        beta.append(beta_all[:, GD_HEADS + hd:GD_HEADS + hd + 1])
        lc_col.append(lc_all[:, hd:hd + 1])
        decay = jnp.exp(lc_col[hd] - lc_all_t[hd:hd + 1, :] + msk_ref[0])
        kb.append(k[hd] * beta[hd])
        p = _mm_nt(jnp.concatenate([q[hd], kb[hd]], axis=0), k[hd])
        qk.append(p[:tb] * decay)
        a = p[tb:] * decay * msk_ref[1]
        t.append(msk_ref[2] - a)
        pw.append(a)
    for _ in range(n_sq):
        for hd in heads:
            pw[hd] = _mm(pw[hd], pw[hd])
            t[hd] = t[hd] + _mm(t[hd], pw[hd])
    u, w, qe = [], [], []
    for hd in heads:
        e_lc = jnp.exp(lc_col[hd])
        uw = _mm(t[hd], jnp.concatenate([v_all[:, sls[hd]] * beta[hd], kb[hd] * e_lc], axis=1))
        u.append(uw[:, :HEAD_W])
        w.append(uw[:, HEAD_W:])
        qe.append(q[hd] * e_lc)
    outs = [[] for _ in heads]
    for c0 in range(0, tb, GD_CHUNK):
        c1 = c0 + GD_CHUNK
        for hd in heads:
            r = _mm(jnp.concatenate([w[hd][c0:c1], qe[hd][c0:c1]], axis=0), s[hd])
            v_new = u[hd][c0:c1] - r[:GD_CHUNK]
            pads = ([jnp.zeros((c0, HEAD_W), F32)] if c0 else []) + [v_new] + (
                [jnp.zeros((tb - c1, HEAD_W), F32)] if c1 < tb else [])
            outs[hd].append(r[GD_CHUNK:] + _mm(qk[hd][c0:c1, :], jnp.concatenate(pads, axis=0)))
            lc_last = lc_col[hd][c1 - 1:c1, :]
            kd = k[hd][c0:c1] * jnp.exp(lc_last - lc_col[hd][c0:c1])
            s[hd] = jnp.exp(lc_last) * s[hd] + _mm_tn(kd, v_new)
    ys = []
    for hd in heads:
        o = jnp.concatenate(outs[hd], axis=0)
        o = o * lax.rsqrt(jnp.mean(o * o, axis=-1, keepdims=True) + RMS_EPS) * nw_ref[...]
        zg = z_ref[:, sls[hd]]
        ys.append((o * (zg * _sigmoid(zg))).astype(o_ref.dtype))
    o_ref[...] = jnp.concatenate(ys, axis=1)
    for hd in heads:
        s_ref[hd] = s[hd]


def _gdn(proj, conv_w, a_log, dt_bias, norm_w, bsz, seq):
    nt = seq // TIME_BLOCK
    idx = np.arange(TIME_BLOCK)
    tri = (idx[:, None] >= idx[None, :]) & (idx[:, None] // GD_CHUNK == idx[None, :] // GD_CHUNK)
    eye = idx[:, None] == idx[None, :]
    masks = jnp.asarray(np.stack([np.where(tri, 0.0, -np.inf), 1.0 - eye, eye]).astype(np.float32))
    tri = jnp.asarray(tri.astype(np.float32), BF16)
    width = GD_HEADS * HEAD_W
    lane_pad = lambda p: jnp.concatenate([p.astype(F32), jnp.zeros((HEAD_W - GD_HEADS,), F32)]
                                         ).reshape(1, HEAD_W)
    col = lambda c0: pl.BlockSpec((TIME_BLOCK, width), lambda b, t: (b * nt + t, c0 // GD_HEADS))
    cw = lambda c0: pl.BlockSpec((CONV_K, width), lambda b, t: (0, c0))
    row = lambda: pl.BlockSpec((1, HEAD_W), lambda b, t: (0, 0))
    return pl.pallas_call(
        _gdn_kernel,
        grid=(bsz, nt),
        in_specs=[col(COL_GQ), col(COL_GK), col(COL_GV), col(COL_GZ),
                  pl.BlockSpec((TIME_BLOCK, HEAD_W), lambda b, t: (b * nt + t, COL_GAB)),
                  cw(0), cw(1), cw(2), row(), row(), row(),
                  pl.BlockSpec((TIME_BLOCK, TIME_BLOCK), lambda b, t: (0, 0)),
                  pl.BlockSpec((3, TIME_BLOCK, TIME_BLOCK), lambda b, t: (0, 0, 0))],
        out_specs=pl.BlockSpec((TIME_BLOCK, width), lambda b, t: (b * nt + t, 0)),
        out_shape=jax.ShapeDtypeStruct((bsz * seq, width), BF16),
        scratch_shapes=[pltpu.VMEM((GD_HEADS, GD_DK, HEAD_W), F32)] + [
            pltpu.VMEM((TIME_BLOCK + 8, width), F32) for _ in range(3)],
        compiler_params=_params(2),
        name="gated_deltanet",
    )(proj, proj, proj, proj, proj, conv_w, conv_w, conv_w, lane_pad(a_log), lane_pad(dt_bias),
      norm_w.reshape(1, HEAD_W), tri, masks)


def _outproj_ln_kernel(yh_ref, yg_ref, wh_ref, wg_ref, x_ref, g_ref, b_ref, o_ref):
    mix = (jnp.dot(yh_ref[...], wh_ref[...], preferred_element_type=F32)
           + jnp.dot(yg_ref[...], wg_ref[...], preferred_element_type=F32))
    o_ref[...] = _layer_norm(ALPHA * x_ref[...] + mix, g_ref[...], b_ref[...])


def _outproj_ln(y_hg, y_gd, w_out, layer, x, ln_g, ln_b):
    n, d = x.shape
    hw = y_hg.shape[1]
    tm = ROW_TILE
    return pl.pallas_call(
        _outproj_ln_kernel,
        grid=(n // tm,),
        in_specs=[pl.BlockSpec((tm, hw), lambda i: (i, 0)),
                  pl.BlockSpec((tm, hw), lambda i: (i, 0)),
                  pl.BlockSpec((None, hw, d), lambda i: (layer, 0, 0)),
                  pl.BlockSpec((None, hw, d), lambda i: (layer, 1, 0)),
                  pl.BlockSpec((tm, d), lambda i: (i, 0)),
                  pl.BlockSpec((None, 1, d), lambda i: (layer * 3, 0, 0)),
                  pl.BlockSpec((None, 1, d), lambda i: (layer * 3, 0, 0))],
        out_specs=pl.BlockSpec((tm, d), lambda i: (i, 0)),
        out_shape=jax.ShapeDtypeStruct((n, d), F32),
        compiler_params=_params(1),
        name="outproj_ln",
    )(y_hg, y_gd, w_out, w_out, x, ln_g, ln_b)


def _attn_router_kernel(x_ref, wq_ref, k_ref, v_ref, wo_ref, g_ref, b_ref, wr_ref, br_ref,
                        o_ref, route_ref):
    x = x_ref[...]
    q = _mm(x, wq_ref[...]).astype(BF16)
    heads = []
    for h in range(MEM_HEADS):
        sl = slice(h * MEM_DH, (h + 1) * MEM_DH)
        s = lax.dot_general(q[:, sl], k_ref[:, sl], NT_DIMS, preferred_element_type=F32)
        s = s * (MEM_DH ** -0.5)
        p = jnp.exp(s - jnp.max(s, axis=-1, keepdims=True))
        p = p / jnp.sum(p, axis=-1, keepdims=True)
        heads.append(_mm(p, v_ref[:, sl]))
    xa = _mm(jnp.concatenate(heads, axis=1), wo_ref[...])
    x2 = _layer_norm(ALPHA * x + xa, g_ref[...], b_ref[...])
    _store_token_tiles(o_ref, x2)

    logits = _mm(x2, wr_ref[...]) + br_ref[...]
    lane = lax.broadcasted_iota(jnp.int32, (1, ROUTE_W), 1).astype(F32)
    neg = -jnp.inf
    first_at = lambda val, top: jnp.min(jnp.where(val == top, lane, float(ROUTE_W)), axis=-1,
                                        keepdims=True)
    gl = jnp.where(lane < N_GROUPS, logits, neg)
    g_max = jnp.max(gl, axis=-1, keepdims=True)
    g_sel = first_at(gl, g_max)
    g_gate = 1.0 / jnp.sum(jnp.exp(gl - g_max), axis=-1, keepdims=True)
    lo = N_GROUPS + EXPERTS_PER_GROUP * g_sel
    el = jnp.where((lane >= lo) & (lane < lo + EXPERTS_PER_GROUP), logits, neg)
    t1 = jnp.max(el, axis=-1, keepdims=True)
    i1 = first_at(el, t1)
    el2 = jnp.where(lane == i1, neg, el)
    t2 = jnp.max(el2, axis=-1, keepdims=True)
    i2 = first_at(el2, t2)
    e2 = jnp.exp(t2 - t1)
    w1 = g_gate / (1.0 + e2)
    w2 = g_gate * e2 / (1.0 + e2)
    route_ref[...] = jnp.where(lane == 0, i1 - N_GROUPS,
                               jnp.where(lane == 1, i2 - N_GROUPS,
                                         jnp.where(lane == 2, w1, jnp.where(lane == 3, w2, 0.0))))


def _attn_router(x, kv_all, w_mq, w_mo, w_route, b_route, layer, ln_g, ln_b, bsz, seq):
    n, d = x.shape
    tq = ROW_TILE
    nq = seq // tq
    mlen = kv_all.shape[0] // bsz
    wspec = lambda: pl.BlockSpec((None, d, d), lambda b, i: (layer, 0, 0))
    lnspec = lambda: pl.BlockSpec((None, 1, d), lambda b, i: (layer * 3 + 1, 0, 0))
    return pl.pallas_call(
        _attn_router_kernel,
        grid=(bsz, nq),
        in_specs=[pl.BlockSpec((tq, d), lambda b, i: (b * nq + i, 0)),
                  wspec(),
                  pl.BlockSpec((mlen, d), lambda b, i: (b, 2 * layer)),
                  pl.BlockSpec((mlen, d), lambda b, i: (b, 2 * layer + 1)),
                  wspec(), lnspec(), lnspec(),
                  pl.BlockSpec((None, d, ROUTE_W), lambda b, i: (layer, 0, 0)),
                  pl.BlockSpec((None, 1, ROUTE_W), lambda b, i: (layer, 0, 0))],
        out_specs=[pl.BlockSpec((tq * ROW_CHUNKS, LANES), lambda b, i: (b * nq + i, 0)),
                   pl.BlockSpec((tq, ROUTE_W), lambda b, i: (b * nq + i, 0))],
        out_shape=[jax.ShapeDtypeStruct((n * ROW_CHUNKS, LANES), F32),
                   jax.ShapeDtypeStruct((n, ROUTE_W), F32)],
        compiler_params=_params(2),
        name="attn_router",
    )(x, w_mq, kv_all, kv_all, w_mo, ln_g, ln_b, w_route, b_route)


def _expert_kernel(order_ref, bstart_ref, bend_ref, be_ref, nu_ref, x_hbm, zeros_hbm, wg_ref, wu_ref,
                   wd_ref, y_hbm, rows0, rows1, yv0, yv1, gsem, ssem, zsem, wg_bf, wu_bf, wd_bf):
    b = pl.program_id(0)
    nu = nu_ref[0]
    m = order_ref.shape[0] - EXPERT_ROWS
    n_tok = m // TOP_K
    k_bits = int(math.log2(TOP_K))
    rows, yv = (rows0, rows1), (yv0, yv1)
    block_rows = EXPERT_ROWS * ROW_CHUNKS

    def tile(r):
        if isinstance(r, int):
            return pl.ds(r * ROW_CHUNKS, ROW_CHUNKS)
        return pl.ds(pl.multiple_of(r * ROW_CHUNKS, ROW_CHUNKS), ROW_CHUNKS)

    def gather_row(base, i, s):
        a = order_ref[base + i]
        pltpu.make_async_copy(x_hbm.at[tile(a >> k_bits), :], rows[s].at[tile(i), :],
                              gsem.at[s]).start()

    def scatter_row(base, n_valid, i, s):
        a = order_ref[base + i]
        row = jnp.where(i < n_valid, (a & (TOP_K - 1)) * n_tok + (a >> k_bits), m + i)
        pltpu.make_async_copy(yv[s].at[tile(i), :], y_hbm.at[tile(row), :], ssem.at[s]).start()

    def wait_gather(s):
        pltpu.make_async_copy(x_hbm.at[pl.ds(0, block_rows), :], rows[s], gsem.at[s]).wait()

    def wait_scatter(s):
        pltpu.make_async_copy(yv[s], y_hbm.at[pl.ds(0, block_rows), :], ssem.at[s]).wait()

    @pl.when(b == 0)
    def _():
        zc = pltpu.make_async_copy(zeros_hbm, y_hbm.at[pl.ds(m * ROW_CHUNKS, block_rows), :], zsem)
        zc.start()
        zc.wait()
        yv1[...] = jnp.zeros_like(yv1)
        base = bstart_ref[0]

        def body(i, carry):
            gather_row(base, i, 0)
            return carry
        lax.fori_loop(0, EXPERT_ROWS, body, 0, unroll=8)

    @pl.when(jnp.logical_or(b == 0, be_ref[b] != be_ref[jnp.maximum(b - 1, 0)]))
    def _():
        wg_bf[...] = wg_ref[...].astype(BF16)
        wu_bf[...] = wu_ref[...].astype(BF16)
        wd_bf[...] = wd_ref[...].astype(BF16)

    def step(s):
        o = 1 - s
        wait_gather(s)
        nxt_base = bstart_ref[jnp.minimum(b + 1, nu - 1)]
        for i in range(EXPERT_ROWS):
            gather_row(nxt_base, i, o)
        prev = jnp.maximum(b - 1, 0)
        prev_base = bstart_ref[prev]
        prev_valid = jnp.where(b == 0, 0, bend_ref[prev] - prev_base)
        for i in range(EXPERT_ROWS):
            scatter_row(prev_base, prev_valid, i, o)

        xb = _load_token_tiles(rows[s], EXPERT_ROWS).astype(BF16)
        hg = jnp.dot(xb, wg_bf[...], preferred_element_type=F32)
        hu = jnp.dot(xb, wu_bf[...], preferred_element_type=F32)
        hid = hg * _sigmoid(hg) * hu
        y = jnp.dot(hid.astype(BF16), wd_bf[...], preferred_element_type=F32)

        @pl.when(b >= 1)
        def _():
            wait_scatter(s)

        _store_token_tiles(yv[s], y)

        @pl.when(b == nu - 1)
        def _():
            base = bstart_ref[b]
            n_valid = bend_ref[b] - base

            def body(i, carry):
                scatter_row(base, n_valid, i, s)
                return carry
            lax.fori_loop(0, EXPERT_ROWS, body, 0, unroll=8)
            wait_gather(o)
            wait_scatter(o)
            wait_scatter(s)

    for parity in range(2):
        @pl.when(jnp.logical_and(b < nu, b % 2 == parity))
        def _():
            step(parity)


def _experts(x, order, bstart, bend, block_expert, n_used, w_gate, w_up, w_down, layer):
    d = D_MODEL
    m = order.shape[0] - EXPERT_ROWS
    block_rows = EXPERT_ROWS * ROW_CHUNKS
    nb = m // EXPERT_ROWS + N_EXPERTS
    any_spec = pl.BlockSpec(memory_space=pl.ANY)
    wspec = lambda r, c: pl.BlockSpec((None, None, r, c),
                                      lambda b, o, bs, bn, be, nu: (layer, be[b], 0, 0))
    grid_spec = pltpu.PrefetchScalarGridSpec(
        num_scalar_prefetch=5, grid=(nb,),
        in_specs=[any_spec, any_spec, wspec(d, D_EXPERT), wspec(d, D_EXPERT), wspec(D_EXPERT, d)],
        out_specs=any_spec,
        scratch_shapes=[pltpu.VMEM((block_rows, LANES), F32) for _ in range(4)] + [
                        pltpu.SemaphoreType.DMA((2,)), pltpu.SemaphoreType.DMA((2,)),
                        pltpu.SemaphoreType.DMA(()),
                        pltpu.VMEM((d, D_EXPERT), BF16), pltpu.VMEM((d, D_EXPERT), BF16),
                        pltpu.VMEM((D_EXPERT, d), BF16)])
    return pl.pallas_call(
        _expert_kernel,
        grid_spec=grid_spec,
        out_shape=jax.ShapeDtypeStruct((m * ROW_CHUNKS + block_rows, LANES), F32),
        compiler_params=_params(1),
        name="moe_experts",
    )(order, bstart, bend, block_expert, n_used, x, jnp.zeros((block_rows, LANES), F32),
      w_gate, w_up, w_down)


def _combine_ln_kernel(y0_ref, y1_ref, x_ref, route_ref, g_ref, b_ref, o_ref):
    route = route_ref[...]
    tm = route.shape[0]
    ff = (route[:, 2:3] * _load_token_tiles(y0_ref, tm)
          + route[:, 3:4] * _load_token_tiles(y1_ref, tm))
    o_ref[...] = _layer_norm(ALPHA * _load_token_tiles(x_ref, tm) + ff, g_ref[...], b_ref[...])


def _combine_ln(y, x, route, layer, ln_g, ln_b):
    n, d = route.shape[0], D_MODEL
    tm = ROW_TILE
    lnspec = lambda: pl.BlockSpec((None, 1, d), lambda i: (layer * 3 + 2, 0, 0))
    tiles = lambda off: pl.BlockSpec((tm * ROW_CHUNKS, LANES), lambda i: (off + i, 0))
    return pl.pallas_call(
        _combine_ln_kernel,
        grid=(n // tm,),
        in_specs=[tiles(0), tiles(n // tm), tiles(0),
                  pl.BlockSpec((tm, ROUTE_W), lambda i: (i, 0)),
                  lnspec(), lnspec()],
        out_specs=pl.BlockSpec((tm, d), lambda i: (i, 0)),
        out_shape=jax.ShapeDtypeStruct((n, d), F32),
        compiler_params=_params(1),
        name="moe_combine_ln",
    )(y, y, x, route, ln_g, ln_b)


def _routing_tables(route, n_tok):
    m = n_tok * TOP_K
    flat_e = route[:, 0:TOP_K].astype(jnp.int32).reshape(m)
    order = jnp.concatenate([jnp.argsort(flat_e).astype(jnp.int32),
                             jnp.zeros((EXPERT_ROWS,), jnp.int32)])
    experts = jnp.arange(N_EXPERTS, dtype=jnp.int32)
    counts = jnp.sum((flat_e[:, None] == experts[None, :]).astype(jnp.int32), axis=0)
    starts = jnp.cumsum(counts) - counts
    pcounts = (counts + EXPERT_ROWS - 1) // EXPERT_ROWS * EXPERT_ROWS
    pends = jnp.cumsum(pcounts)
    n_blocks = m // EXPERT_ROWS + N_EXPERTS
    n_used = pends[-1] // EXPERT_ROWS
    blk = jnp.minimum(jnp.arange(n_blocks, dtype=jnp.int32), n_used - 1)
    block_expert = jnp.minimum(jnp.searchsorted(pends, blk * EXPERT_ROWS, side='right'),
                               N_EXPERTS - 1).astype(jnp.int32)
    bstart = starts[block_expert] + blk * EXPERT_ROWS - (pends - pcounts)[block_expert]
    bend = (starts + counts)[block_expert]
    return (order, bstart.astype(jnp.int32), bend.astype(jnp.int32), block_expert,
            n_used.reshape(1).astype(jnp.int32))


def kernel(x, mem, w_in, hg_lb_logits, hg_norm_w, gd_conv_w, gd_a_log, gd_dt_bias, gd_norm_w, w_out,
           mem_ln_g, mem_ln_b, w_mq, w_mk, w_mv, w_mo, w_group, b_group, w_router, b_router,
           w_gate, w_up, w_down, ln_g, ln_b):
    bsz, seq, d = x.shape
    n = bsz * seq
    lb_all = jnp.cumsum(jax.nn.softmax(hg_lb_logits.astype(F32), axis=0), axis=0)
    lb_all = lb_all - lb_all[0]
    lb_tab = jnp.stack([jnp.log(lb_all), jnp.log1p(-lb_all), 1.0 - lb_all], axis=1)
    sp = np.cumsum((512, 512, 512, 512, 512, 512, 512, GD_HEADS, GD_HEADS, 512))
    seg = lambda a, b: w_in[:, :, (sp[a - 1] if a else 0):sp[b]]
    w_in_r = jnp.concatenate(
        [seg(0, 6), seg(9, 9), seg(7, 8), jnp.zeros((DEPTH, d, HEAD_W - 2 * GD_HEADS), F32)],
        axis=-1).astype(BF16)
    w_out_b = w_out.astype(BF16)
    w_mq_b, w_mo_b = w_mq.astype(BF16), w_mo.astype(BF16)
    w_kv = jnp.stack([w_mk, w_mv], axis=1).astype(BF16)
    w_kv = w_kv.transpose(2, 0, 1, 3).reshape(d, DEPTH * 2 * d)
    pad = ROUTE_W - N_GROUPS - N_EXPERTS
    w_route = jnp.concatenate([w_group, w_router, jnp.zeros((DEPTH, d, pad), F32)], axis=-1).astype(BF16)
    b_route = jnp.concatenate([b_group, b_router, jnp.zeros((DEPTH, pad), F32)], axis=-1)
    b_route = b_route.reshape(DEPTH, 1, ROUTE_W).astype(F32)
    ln_g3 = ln_g.reshape(DEPTH * 3, 1, d)
    ln_b3 = ln_b.reshape(DEPTH * 3, 1, d)

    mem_n = _ln(mem.reshape(-1, d), mem_ln_g, mem_ln_b, ROW_TILE)
    kv_all = _matmul(mem_n, w_kv, BF16, ROW_TILE, 2 * d)

    xt = x.reshape(n, d)
    for l in range(DEPTH):
        proj = _matmul(xt, w_in_r[l], F32, ROW_TILE // 2, PROJ_W)
        y_hg = _hgrn(proj, lb_tab[l], hg_norm_w[l], bsz, seq)
        y_gd = _gdn(proj, gd_conv_w[l], gd_a_log[l], gd_dt_bias[l], gd_norm_w[l], bsz, seq)
        x1 = _outproj_ln(y_hg, y_gd, w_out_b, l, xt, ln_g3, ln_b3)
        x2, route = _attn_router(x1, kv_all, w_mq_b, w_mo_b, w_route, b_route, l, ln_g3, ln_b3,
                                 bsz, seq)
        order, bstart, bend, block_expert, n_used = _routing_tables(route, n)
        y = _experts(x2, order, bstart, bend, block_expert, n_used, w_gate, w_up, w_down, l)
        xt = _combine_ln(y, x2, route, l, ln_g3, ln_b3)
    return xt.reshape(bsz, seq, d)
```

```python
import functools
import math

import jax
import jax.numpy as jnp
import numpy as np
from jax import lax
from jax.experimental import pallas as pl
from jax.experimental.pallas import tpu as pltpu

D_MODEL = 1024
DEPTH = 4
HG_HEADS = 4
HG_DK = 128
GD_HEADS = 4
GD_DK = 128
HEAD_W = 128
CONV_K = 4
MEM_HEADS = 4
MEM_DH = D_MODEL // MEM_HEADS
N_GROUPS = 4
EXPERTS_PER_GROUP = 8
N_EXPERTS = N_GROUPS * EXPERTS_PER_GROUP
TOP_K = 2
D_EXPERT = 512
ALPHA = (2.0 * DEPTH) ** 0.25
LN_EPS = 1e-5
RMS_EPS = 1e-6
L2_EPS = 1e-6

COL_HQ, COL_HF, COL_HI, COL_HZ, COL_GQ, COL_GK, COL_GV, COL_GZ, COL_GAB = 0, 4, 8, 12, 16, 20, 24, 28, 32
PROJ_W = 33 * HEAD_W

TIME_BLOCK = 256
GD_CHUNK = 64
ROW_TILE = 512
EXPERT_ROWS = 256
ROUTE_W = 128
LANES = 128
ROW_CHUNKS = D_MODEL // LANES
VMEM_LIMIT_BYTES = 48 * 1024 * 1024

BF16 = jnp.bfloat16
F32 = jnp.float32
NT_DIMS = (((1,), (1,)), ((), ()))
TN_DIMS = (((0,), (0,)), ((), ()))


def _params(n_grid):
    return pltpu.CompilerParams(dimension_semantics=("arbitrary",) * n_grid,
                                vmem_limit_bytes=VMEM_LIMIT_BYTES)


def _mm(a, b):
    return jnp.dot(a.astype(BF16), b.astype(BF16), preferred_element_type=F32)


def _mm_nt(a, b):
    return lax.dot_general(a.astype(BF16), b.astype(BF16), NT_DIMS, preferred_element_type=F32)


def _mm_tn(a, b):
    return lax.dot_general(a.astype(BF16), b.astype(BF16), TN_DIMS, preferred_element_type=F32)


def _cumsum_rows(tri, x):
    p0 = x.astype(BF16)
    r0 = x - p0.astype(F32)
    p1 = r0.astype(BF16)
    p2 = (r0 - p1.astype(F32)).astype(BF16)
    dot = lambda p: jnp.dot(tri, p, preferred_element_type=F32)
    return dot(p0) + dot(p1) + dot(p2)


def _sigmoid(x):
    e = jnp.exp(-jnp.abs(x))
    return jnp.where(x >= 0, 1.0, e) / (1.0 + e)


def _softplus(x):
    return jnp.maximum(x, 0.0) + jnp.log1p(jnp.exp(-jnp.abs(x)))


def _load_token_tiles(ref, n):
    return jnp.concatenate([ref[pl.ds(j, n, stride=ROW_CHUNKS), :] for j in range(ROW_CHUNKS)],
                           axis=1)


def _store_token_tiles(ref, x):
    for j in range(ROW_CHUNKS):
        ref[pl.ds(j, x.shape[0], stride=ROW_CHUNKS), :] = x[:, j * LANES:(j + 1) * LANES]


def _layer_norm(z, g, b):
    mu = jnp.mean(z, axis=-1, keepdims=True)
    zc = z - mu
    var = jnp.mean(zc * zc, axis=-1, keepdims=True)
    return zc * lax.rsqrt(var + LN_EPS) * g + b


def _matmul_kernel(x_ref, w_ref, o_ref):
    o_ref[...] = _mm(x_ref[...], w_ref[...]).astype(o_ref.dtype)


def _matmul(x, w, out_dtype, tm, tn):
    m, k = x.shape
    n = w.shape[1]
    return pl.pallas_call(
        _matmul_kernel,
        grid=(m // tm, n // tn),
        in_specs=[pl.BlockSpec((tm, k), lambda i, j: (i, 0)),
                  pl.BlockSpec((k, tn), lambda i, j: (0, j))],
        out_specs=pl.BlockSpec((tm, tn), lambda i, j: (i, j)),
        out_shape=jax.ShapeDtypeStruct((m, n), out_dtype),
        compiler_params=_params(2),
        name="matmul",
    )(x, w)


def _ln_kernel(x_ref, g_ref, b_ref, o_ref):
    o_ref[...] = _layer_norm(x_ref[...], g_ref[...], b_ref[...])


def _ln(x, g, b, tm):
    n, d = x.shape
    return pl.pallas_call(
        _ln_kernel,
        grid=(n // tm,),
        in_specs=[pl.BlockSpec((tm, d), lambda i: (i, 0)),
                  pl.BlockSpec((1, d), lambda i: (0, 0)),
                  pl.BlockSpec((1, d), lambda i: (0, 0))],
        out_specs=pl.BlockSpec((tm, d), lambda i: (i, 0)),
        out_shape=jax.ShapeDtypeStruct((n, d), F32),
        compiler_params=_params(1),
        name="mem_ln",
    )(x, g.reshape(1, d), b.reshape(1, d))


def _ref_rows(cum, row, h):
    c, w = cum.shape
    if 2 * h <= 8:
        j = row & (2 * h - 1)
        m = cum
        for dl in range(-(h - 1), h + 1):
            if dl != 0:
                m = jnp.where(j - (h - 1) == dl, pltpu.roll(cum, dl % c, axis=0), m)
        return m
    pieces = [jnp.broadcast_to(cum[s + h - 1:s + h, :], (2 * h, w)) for s in range(0, c, 2 * h)]
    return jnp.concatenate(pieces, axis=0) if len(pieces) > 1 else pieces[0]


def _hgrn_kernel(q_ref, f_ref, i_ref, z_ref, lb_ref, nw_ref, tri_ref, msk_ref, o_ref, st_ref):
    @pl.when(pl.program_id(1) == 0)
    def _():
        st_ref[...] = jnp.zeros_like(st_ref)

    c = q_ref.shape[0]
    zf = f_ref[...]
    log_lb, log1m_lb, one_m_lb = lb_ref[0:1, :], lb_ref[1:2, :], lb_ref[2:3, :]
    e = jnp.exp(-jnp.abs(zf))
    log_sig = jnp.minimum(zf, 0.0) - jnp.log1p(e)
    b = log1m_lb + log_sig
    log_f = jnp.maximum(log_lb, b) + jnp.log1p(jnp.exp(-jnp.abs(log_lb - b)))
    k_all = one_m_lb * (jnp.where(zf >= 0, e, 1.0) / (1.0 + e))
    cum_all = _cumsum_rows(tri_ref[...], log_f)

    row = lax.broadcasted_iota(jnp.int32, (c, 1), 0)
    n_lvl = msk_ref.shape[0] - 1
    states = [st_ref[hd] for hd in range(HG_HEADS)]
    ys = []
    for hd in range(HG_HEADS):
        sl = slice(hd * HEAD_W, (hd + 1) * HEAD_W)
        q, k, v, cum = q_ref[:, sl], k_all[:, sl], i_ref[:, sl], cum_all[:, sl]
        scores = msk_ref[n_lvl] * jnp.sum(q * k, axis=-1, keepdims=True)
        for lvl in range(n_lvl):
            h = 1 << lvl
            x = jnp.exp(-jnp.abs(cum - _ref_rows(cum, row, h)))
            z = jnp.where((row & (2 * h - 1)) >= h, q, k) * x
            scores = scores + _mm_nt(z, z) * msk_ref[lvl]

        st = states[hd]
        o = _mm_nt(q * jnp.exp(cum), st) + _mm(scores, v)
        cum_last = cum[c - 1:c, :]
        states[hd] = jnp.exp(cum_last) * st + _mm_tn(v, k * jnp.exp(cum_last - cum))
        o = o * lax.rsqrt(jnp.mean(o * o, axis=-1, keepdims=True) + RMS_EPS) * nw_ref[...]
        ys.append((o * _sigmoid(z_ref[:, sl])).astype(o_ref.dtype))
    o_ref[...] = jnp.concatenate(ys, axis=1)
    for hd in range(HG_HEADS):
        st_ref[hd] = states[hd]


def _hgrn(proj, lb_tab, norm_w, bsz, seq):
    nt = seq // TIME_BLOCK
    width = HG_HEADS * HEAD_W
    idx = np.arange(TIME_BLOCK)
    tri = jnp.asarray((idx[:, None] >= idx[None, :]).astype(np.float32), BF16)
    n_lvl = int(math.log2(TIME_BLOCK))
    half = lambda l: (idx >> l) & 1
    masks = [((idx[:, None] >> (l + 1)) == (idx[None, :] >> (l + 1)))
             & (half(l)[:, None] == 1) & (half(l)[None, :] == 0) for l in range(n_lvl)]
    masks = jnp.asarray(np.stack(masks + [idx[:, None] == idx[None, :]]).astype(np.float32))
    col = lambda c0: pl.BlockSpec((TIME_BLOCK, width), lambda b, t: (b * nt + t, c0 // HG_HEADS))
    return pl.pallas_call(
        _hgrn_kernel,
        grid=(bsz, nt),
        in_specs=[col(COL_HQ), col(COL_HF), col(COL_HI), col(COL_HZ),
                  pl.BlockSpec((3, width), lambda b, t: (0, 0)),
                  pl.BlockSpec((1, HEAD_W), lambda b, t: (0, 0)),
                  pl.BlockSpec((TIME_BLOCK, TIME_BLOCK), lambda b, t: (0, 0)),
                  pl.BlockSpec((n_lvl + 1, TIME_BLOCK, TIME_BLOCK), lambda b, t: (0, 0, 0))],
        out_specs=pl.BlockSpec((TIME_BLOCK, width), lambda b, t: (b * nt + t, 0)),
        out_shape=jax.ShapeDtypeStruct((bsz * seq, width), BF16),
        scratch_shapes=[pltpu.VMEM((HG_HEADS, HEAD_W, HG_DK), F32)],
        compiler_params=_params(2),
        name="hgrn2",
    )(proj, proj, proj, proj, lb_tab, norm_w.reshape(1, HEAD_W), tri, masks)


def _conv_silu(x_ref, w_ref, buf_ref, first):
    tb = x_ref.shape[0]

    @pl.when(first)
    def _():
        buf_ref[0:8, :] = jnp.zeros((8, buf_ref.shape[1]), F32)

    @pl.when(jnp.logical_not(first))
    def _():
        buf_ref[0:8, :] = buf_ref[tb:tb + 8, :]

    buf_ref[8:tb + 8, :] = x_ref[...]
    y = w_ref[CONV_K - 1:CONV_K, :] * x_ref[...]
    for j in range(1, CONV_K):
        y = y + w_ref[CONV_K - 1 - j:CONV_K - j, :] * buf_ref[8 - j:8 - j + tb, :]
    return y * _sigmoid(y)


def _gdn_kernel(q_ref, k_ref, v_ref, z_ref, ab_ref, wq_ref, wk_ref, wv_ref, alog_ref, dtb_ref,
                nw_ref, tri_ref, msk_ref, o_ref, s_ref, bq_ref, bk_ref, bv_ref):
    first = pl.program_id(1) == 0

    @pl.when(first)
    def _():
        s_ref[...] = jnp.zeros_like(s_ref)

    tb = q_ref.shape[0]
    cq_all = _conv_silu(q_ref, wq_ref, bq_ref, first)
    ck_all = _conv_silu(k_ref, wk_ref, bk_ref, first)
    v_all = _conv_silu(v_ref, wv_ref, bv_ref, first)

    lane = lax.broadcasted_iota(jnp.int32, (1, HEAD_W), 1)
    ab = ab_ref[...]
    g_all = jnp.where(lane < GD_HEADS, -jnp.exp(alog_ref[...]) * _softplus(ab + dtb_ref[...]), 0.0)
    beta_all = _sigmoid(ab)
    lc_all = _cumsum_rows(tri_ref[...], g_all)
    lc_all_t = jnp.transpose(lc_all)
    n_sq = int(math.log2(GD_CHUNK)) - 1

    heads = range(GD_HEADS)
    sls = [slice(hd * HEAD_W, (hd + 1) * HEAD_W) for hd in heads]
    s = [s_ref[hd] for hd in heads]
    q, k, kb, beta, lc_col, qk, t, pw = [], [], [], [], [], [], [], []
    for hd in heads:
        cq, ck = cq_all[:, sls[hd]], ck_all[:, sls[hd]]
        q.append(cq * lax.rsqrt(jnp.sum(cq * cq, axis=-1, keepdims=True) + L2_EPS) * (GD_DK ** -0.5))
        k.append(ck * lax.rsqrt(jnp.sum(ck * ck, axis=-1, keepdims=True) + L2_EPS))
        beta.append(beta_all[:, GD_HEADS + hd:GD_HEADS + hd + 1])
        lc_col.append(lc_all[:, hd:hd + 1])
        decay = jnp.exp(lc_col[hd] - lc_all_t[hd:hd + 1, :] + msk_ref[0])
        kb.append(k[hd] * beta[hd])
        p = _mm_nt(jnp.concatenate([q[hd], kb[hd]], axis=0), k[hd])
        qk.append(p[:tb] * decay)
        a = p[tb:] * decay * msk_ref[1]
        t.append(msk_ref[2] - a)
        pw.append(a)
    for _ in range(n_sq):
        for hd in heads:
            pw[hd] = _mm(pw[hd], pw[hd])
            t[hd] = t[hd] + _mm(t[hd], pw[hd])
    u, w, qe = [], [], []
    for hd in heads:
        e_lc = jnp.exp(lc_col[hd])
        uw = _mm(t[hd], jnp.concatenate([v_all[:, sls[hd]] * beta[hd], kb[hd] * e_lc], axis=1))
        u.append(uw[:, :HEAD_W])
        w.append(uw[:, HEAD_W:])
        qe.append(q[hd] * e_lc)
    outs = [[] for _ in heads]
    for c0 in range(0, tb, GD_CHUNK):
        c1 = c0 + GD_CHUNK
        for hd in heads:
            r = _mm(jnp.concatenate([w[hd][c0:c1], qe[hd][c0:c1]], axis=0), s[hd])
            v_new = u[hd][c0:c1] - r[:GD_CHUNK]
            pads = ([jnp.zeros((c0, HEAD_W), F32)] if c0 else []) + [v_new] + (
                [jnp.zeros((tb - c1, HEAD_W), F32)] if c1 < tb else [])
            outs[hd].append(r[GD_CHUNK:] + _mm(qk[hd][c0:c1, :], jnp.concatenate(pads, axis=0)))
            lc_last = lc_col[hd][c1 - 1:c1, :]
            kd = k[hd][c0:c1] * jnp.exp(lc_last - lc_col[hd][c0:c1])
            s[hd] = jnp.exp(lc_last) * s[hd] + _mm_tn(kd, v_new)
    ys = []
    for hd in heads:
        o = jnp.concatenate(outs[hd], axis=0)
        o = o * lax.rsqrt(jnp.mean(o * o, axis=-1, keepdims=True) + RMS_EPS) * nw_ref[...]
        zg = z_ref[:, sls[hd]]
        ys.append((o * (zg * _sigmoid(zg))).astype(o_ref.dtype))
    o_ref[...] = jnp.concatenate(ys, axis=1)
    for hd in heads:
        s_ref[hd] = s[hd]


def _gdn(proj, conv_w, a_log, dt_bias, norm_w, bsz, seq):
    nt = seq // TIME_BLOCK
    idx = np.arange(TIME_BLOCK)
    tri = (idx[:, None] >= idx[None, :]) & (idx[:, None] // GD_CHUNK == idx[None, :] // GD_CHUNK)
    eye = idx[:, None] == idx[None, :]
    masks = jnp.asarray(np.stack([np.where(tri, 0.0, -np.inf), 1.0 - eye, eye]).astype(np.float32))
    tri = jnp.asarray(tri.astype(np.float32), BF16)
    width = GD_HEADS * HEAD_W
    lane_pad = lambda p: jnp.concatenate([p.astype(F32), jnp.zeros((HEAD_W - GD_HEADS,), F32)]
                                         ).reshape(1, HEAD_W)
    col = lambda c0: pl.BlockSpec((TIME_BLOCK, width), lambda b, t: (b * nt + t, c0 // GD_HEADS))
    cw = lambda c0: pl.BlockSpec((CONV_K, width), lambda b, t: (0, c0))
    row = lambda: pl.BlockSpec((1, HEAD_W), lambda b, t: (0, 0))
    return pl.pallas_call(
        _gdn_kernel,
        grid=(bsz, nt),
        in_specs=[col(COL_GQ), col(COL_GK), col(COL_GV), col(COL_GZ),
                  pl.BlockSpec((TIME_BLOCK, HEAD_W), lambda b, t: (b * nt + t, COL_GAB)),
                  cw(0), cw(1), cw(2), row(), row(), row(),
                  pl.BlockSpec((TIME_BLOCK, TIME_BLOCK), lambda b, t: (0, 0)),
                  pl.BlockSpec((3, TIME_BLOCK, TIME_BLOCK), lambda b, t: (0, 0, 0))],
        out_specs=pl.BlockSpec((TIME_BLOCK, width), lambda b, t: (b * nt + t, 0)),
        out_shape=jax.ShapeDtypeStruct((bsz * seq, width), BF16),
        scratch_shapes=[pltpu.VMEM((GD_HEADS, GD_DK, HEAD_W), F32)] + [
            pltpu.VMEM((TIME_BLOCK + 8, width), F32) for _ in range(3)],
        compiler_params=_params(2),
        name="gated_deltanet",
    )(proj, proj, proj, proj, proj, conv_w, conv_w, conv_w, lane_pad(a_log), lane_pad(dt_bias),
      norm_w.reshape(1, HEAD_W), tri, masks)


def _attn_router_kernel(yh_ref, yg_ref, wh_ref, wg_ref, x0_ref, g0_ref, b0_ref, wq_ref, k_ref, v_ref,
                        wo_ref, g_ref, b_ref, wr_ref, br_ref, o_ref, route_ref):
    half = x0_ref.shape[0] // 2
    parts = [slice(0, half), slice(half, 2 * half)]
    x = []
    for rs in parts:
        mix = (jnp.dot(yh_ref[rs, :], wh_ref[...], preferred_element_type=F32)
               + jnp.dot(yg_ref[rs, :], wg_ref[...], preferred_element_type=F32))
        x.append(_layer_norm(ALPHA * x0_ref[rs, :] + mix, g0_ref[...], b0_ref[...]))
    q = [_mm(xi, wq_ref[...]).astype(BF16) for xi in x]
    heads = [[] for _ in parts]
    for h in range(MEM_HEADS):
        sl = slice(h * MEM_DH, (h + 1) * MEM_DH)
        for j in range(len(parts)):
            s = lax.dot_general(q[j][:, sl], k_ref[:, sl], NT_DIMS, preferred_element_type=F32)
            s = s * (MEM_DH ** -0.5)
            p = jnp.exp(s - jnp.max(s, axis=-1, keepdims=True))
            p = p / jnp.sum(p, axis=-1, keepdims=True)
            heads[j].append(_mm(p, v_ref[:, sl]))
    x2 = [_layer_norm(ALPHA * x[j] + _mm(jnp.concatenate(heads[j], axis=1), wo_ref[...]),
                      g_ref[...], b_ref[...]) for j in range(len(parts))]
    for j, rs in enumerate(parts):
        _store_token_tiles(o_ref.at[pl.ds(rs.start * ROW_CHUNKS, half * ROW_CHUNKS), :], x2[j])
        route_ref[rs, :] = _route_select(_mm(x2[j], wr_ref[...]) + br_ref[...])


def _route_select(logits):
    lane = lax.broadcasted_iota(jnp.int32, (1, ROUTE_W), 1).astype(F32)
    neg = -jnp.inf
    first_at = lambda val, top: jnp.min(jnp.where(val == top, lane, float(ROUTE_W)), axis=-1,
                                        keepdims=True)
    gl = jnp.where(lane < N_GROUPS, logits, neg)
    g_max = jnp.max(gl, axis=-1, keepdims=True)
    g_sel = first_at(gl, g_max)
    g_gate = 1.0 / jnp.sum(jnp.exp(gl - g_max), axis=-1, keepdims=True)
    lo = N_GROUPS + EXPERTS_PER_GROUP * g_sel
    el = jnp.where((lane >= lo) & (lane < lo + EXPERTS_PER_GROUP), logits, neg)
    t1 = jnp.max(el, axis=-1, keepdims=True)
    i1 = first_at(el, t1)
    el2 = jnp.where(lane == i1, neg, el)
    t2 = jnp.max(el2, axis=-1, keepdims=True)
    i2 = first_at(el2, t2)
    e2 = jnp.exp(t2 - t1)
    w1 = g_gate / (1.0 + e2)
    w2 = g_gate * e2 / (1.0 + e2)
    return jnp.where(lane == 0, i1 - N_GROUPS,
                     jnp.where(lane == 1, i2 - N_GROUPS,
                               jnp.where(lane == 2, w1, jnp.where(lane == 3, w2, 0.0))))


def _attn_router(y_hg, y_gd, w_out, x, kv_all, w_mq, w_mo, w_route, b_route, layer, ln_g, ln_b,
                 bsz, seq):
    n, d = x.shape
    hw = y_hg.shape[1]
    tq = ROW_TILE
    nq = seq // tq
    mlen = kv_all.shape[0] // bsz
    wspec = lambda: pl.BlockSpec((None, d, d), lambda b, i: (layer, 0, 0))
    lnspec = lambda j: pl.BlockSpec((None, 1, d), lambda b, i: (layer * 3 + j, 0, 0))
    yspec = lambda: pl.BlockSpec((tq, hw), lambda b, i: (b * nq + i, 0))
    return pl.pallas_call(
        _attn_router_kernel,
        grid=(bsz, nq),
        in_specs=[yspec(), yspec(),
                  pl.BlockSpec((None, hw, d), lambda b, i: (layer, 0, 0)),
                  pl.BlockSpec((None, hw, d), lambda b, i: (layer, 1, 0)),
                  pl.BlockSpec((tq, d), lambda b, i: (b * nq + i, 0)),
                  lnspec(0), lnspec(0),
                  wspec(),
                  pl.BlockSpec((mlen, d), lambda b, i: (b, 2 * layer)),
                  pl.BlockSpec((mlen, d), lambda b, i: (b, 2 * layer + 1)),
                  wspec(), lnspec(1), lnspec(1),
                  pl.BlockSpec((None, d, ROUTE_W), lambda b, i: (layer, 0, 0)),
                  pl.BlockSpec((None, 1, ROUTE_W), lambda b, i: (layer, 0, 0))],
        out_specs=[pl.BlockSpec((tq * ROW_CHUNKS, LANES), lambda b, i: (b * nq + i, 0)),
                   pl.BlockSpec((tq, ROUTE_W), lambda b, i: (b * nq + i, 0))],
        out_shape=[jax.ShapeDtypeStruct((n * ROW_CHUNKS, LANES), F32),
                   jax.ShapeDtypeStruct((n, ROUTE_W), F32)],
        compiler_params=_params(2),
        name="attn_router",
    )(y_hg, y_gd, w_out, w_out, x, ln_g, ln_b, w_mq, kv_all, kv_all, w_mo, ln_g, ln_b, w_route,
      b_route)


def _expert_kernel(order_ref, bstart_ref, bend_ref, be_ref, nu_ref, x_hbm, zeros_hbm, wg_ref, wu_ref,
                   wd_ref, y_hbm, rows0, rows1, yv0, yv1, gsem, ssem, zsem, wg_bf, wu_bf, wd_bf):
    b = pl.program_id(0)
    nu = nu_ref[0]
    m = order_ref.shape[0] - EXPERT_ROWS
    n_tok = m // TOP_K
    k_bits = int(math.log2(TOP_K))
    rows, yv = (rows0, rows1), (yv0, yv1)
    block_rows = EXPERT_ROWS * ROW_CHUNKS

    def tile(r):
        if isinstance(r, int):
            return pl.ds(r * ROW_CHUNKS, ROW_CHUNKS)
        return pl.ds(pl.multiple_of(r * ROW_CHUNKS, ROW_CHUNKS), ROW_CHUNKS)

    def gather_row(base, i, s):
        a = order_ref[base + i]
        pltpu.make_async_copy(x_hbm.at[tile(a >> k_bits), :], rows[s].at[tile(i), :],
                              gsem.at[s]).start()

    def scatter_row(base, n_valid, i, s):
        a = order_ref[base + i]
        row = jnp.where(i < n_valid, (a & (TOP_K - 1)) * n_tok + (a >> k_bits), m + i)
        pltpu.make_async_copy(yv[s].at[tile(i), :], y_hbm.at[tile(row), :], ssem.at[s]).start()

    def wait_gather(s):
        pltpu.make_async_copy(x_hbm.at[pl.ds(0, block_rows), :], rows[s], gsem.at[s]).wait()

    def wait_scatter(s):
        pltpu.make_async_copy(yv[s], y_hbm.at[pl.ds(0, block_rows), :], ssem.at[s]).wait()

    @pl.when(b == 0)
    def _():
        zc = pltpu.make_async_copy(zeros_hbm, y_hbm.at[pl.ds(m * ROW_CHUNKS, block_rows), :], zsem)
        zc.start()
        zc.wait()
        yv1[...] = jnp.zeros_like(yv1)
        base = bstart_ref[0]

        def body(i, carry):
            gather_row(base, i, 0)
            return carry
        lax.fori_loop(0, EXPERT_ROWS, body, 0, unroll=8)

    @pl.when(jnp.logical_or(b == 0, be_ref[b] != be_ref[jnp.maximum(b - 1, 0)]))
    def _():
        wg_bf[...] = wg_ref[...].astype(BF16)
        wu_bf[...] = wu_ref[...].astype(BF16)
        wd_bf[...] = wd_ref[...].astype(BF16)

    def step(s):
        o = 1 - s
        wait_gather(s)
        nxt_base = bstart_ref[jnp.minimum(b + 1, nu - 1)]
        for i in range(EXPERT_ROWS):
            gather_row(nxt_base, i, o)
        prev = jnp.maximum(b - 1, 0)
        prev_base = bstart_ref[prev]
        prev_valid = jnp.where(b == 0, 0, bend_ref[prev] - prev_base)
        for i in range(EXPERT_ROWS):
            scatter_row(prev_base, prev_valid, i, o)

        xb = _load_token_tiles(rows[s], EXPERT_ROWS).astype(BF16)
        hg = jnp.dot(xb, wg_bf[...], preferred_element_type=F32)
        hu = jnp.dot(xb, wu_bf[...], preferred_element_type=F32)
        hid = hg * _sigmoid(hg) * hu
        y = jnp.dot(hid.astype(BF16), wd_bf[...], preferred_element_type=F32)

        @pl.when(b >= 1)
        def _():
            wait_scatter(s)

        _store_token_tiles(yv[s], y)

        @pl.when(b == nu - 1)
        def _():
            base = bstart_ref[b]
            n_valid = bend_ref[b] - base

            def body(i, carry):
                scatter_row(base, n_valid, i, s)
                return carry
            lax.fori_loop(0, EXPERT_ROWS, body, 0, unroll=8)
            wait_gather(o)
            wait_scatter(o)
            wait_scatter(s)

    for parity in range(2):
        @pl.when(jnp.logical_and(b < nu, b % 2 == parity))
        def _():
            step(parity)


def _experts(x, order, bstart, bend, block_expert, n_used, w_gate, w_up, w_down, layer):
    d = D_MODEL
    m = order.shape[0] - EXPERT_ROWS
    block_rows = EXPERT_ROWS * ROW_CHUNKS
    nb = m // EXPERT_ROWS + N_EXPERTS
    any_spec = pl.BlockSpec(memory_space=pl.ANY)
    wspec = lambda r, c: pl.BlockSpec((None, None, r, c),
                                      lambda b, o, bs, bn, be, nu: (layer, be[b], 0, 0))
    grid_spec = pltpu.PrefetchScalarGridSpec(
        num_scalar_prefetch=5, grid=(nb,),
        in_specs=[any_spec, any_spec, wspec(d, D_EXPERT), wspec(d, D_EXPERT), wspec(D_EXPERT, d)],
        out_specs=any_spec,
        scratch_shapes=[pltpu.VMEM((block_rows, LANES), F32) for _ in range(4)] + [
                        pltpu.SemaphoreType.DMA((2,)), pltpu.SemaphoreType.DMA((2,)),
                        pltpu.SemaphoreType.DMA(()),
                        pltpu.VMEM((d, D_EXPERT), BF16), pltpu.VMEM((d, D_EXPERT), BF16),
                        pltpu.VMEM((D_EXPERT, d), BF16)])
    return pl.pallas_call(
        _expert_kernel,
        grid_spec=grid_spec,
        out_shape=jax.ShapeDtypeStruct((m * ROW_CHUNKS + block_rows, LANES), F32),
        compiler_params=_params(1),
        name="moe_experts",
    )(order, bstart, bend, block_expert, n_used, x, jnp.zeros((block_rows, LANES), F32),
      w_gate, w_up, w_down)


def _combine_ln_kernel(y0_ref, y1_ref, x_ref, route_ref, g_ref, b_ref, *rest):
    route = route_ref[...]
    tm = route.shape[0]
    ff = (route[:, 2:3] * _load_token_tiles(y0_ref, tm)
          + route[:, 3:4] * _load_token_tiles(y1_ref, tm))
    xn = _layer_norm(ALPHA * _load_token_tiles(x_ref, tm) + ff, g_ref[...], b_ref[...])
    if len(rest) == 1:
        (o_ref,) = rest
    else:
        w_ref, o_ref, proj_ref = rest
        proj_ref[...] = _mm(xn, w_ref[...])
    o_ref[...] = xn


def _combine_ln(y, x, route, layer, ln_g, ln_b, w_next=None):
    n, d = route.shape[0], D_MODEL
    tm = ROW_TILE if w_next is None else ROW_TILE // 2
    lnspec = lambda: pl.BlockSpec((None, 1, d), lambda i: (layer * 3 + 2, 0, 0))
    tiles = lambda off: pl.BlockSpec((tm * ROW_CHUNKS, LANES), lambda i: (off + i, 0))
    in_specs = [tiles(0), tiles(n // tm), tiles(0), pl.BlockSpec((tm, ROUTE_W), lambda i: (i, 0)),
                lnspec(), lnspec()]
    out_specs = pl.BlockSpec((tm, d), lambda i: (i, 0))
    out_shape = jax.ShapeDtypeStruct((n, d), F32)
    args = (y, y, x, route, ln_g, ln_b)
    if w_next is not None:
        pw = w_next.shape[1]
        in_specs.append(pl.BlockSpec((d, pw), lambda i: (0, 0)))
        out_specs = [out_specs, pl.BlockSpec((tm, pw), lambda i: (i, 0))]
        out_shape = [out_shape, jax.ShapeDtypeStruct((n, pw), F32)]
        args += (w_next,)
    return pl.pallas_call(
        _combine_ln_kernel,
        grid=(n // tm,),
        in_specs=in_specs,
        out_specs=out_specs,
        out_shape=out_shape,
        compiler_params=_params(1),
        name="moe_combine_ln",
    )(*args)


def _routing_tables(route, n_tok):
    m = n_tok * TOP_K
    flat_e = route[:, 0:TOP_K].astype(jnp.int32).reshape(m)
    order = jnp.concatenate([jnp.argsort(flat_e).astype(jnp.int32),
                             jnp.zeros((EXPERT_ROWS,), jnp.int32)])
    experts = jnp.arange(N_EXPERTS, dtype=jnp.int32)
    counts = jnp.sum((flat_e[:, None] == experts[None, :]).astype(jnp.int32), axis=0)
    starts = jnp.cumsum(counts) - counts
    pcounts = (counts + EXPERT_ROWS - 1) // EXPERT_ROWS * EXPERT_ROWS
    pends = jnp.cumsum(pcounts)
    n_blocks = m // EXPERT_ROWS + N_EXPERTS
    n_used = pends[-1] // EXPERT_ROWS
    blk = jnp.minimum(jnp.arange(n_blocks, dtype=jnp.int32), n_used - 1)
    block_expert = jnp.minimum(jnp.searchsorted(pends, blk * EXPERT_ROWS, side='right'),
                               N_EXPERTS - 1).astype(jnp.int32)
    bstart = starts[block_expert] + blk * EXPERT_ROWS - (pends - pcounts)[block_expert]
    bend = (starts + counts)[block_expert]
    return (order, bstart.astype(jnp.int32), bend.astype(jnp.int32), block_expert,
            n_used.reshape(1).astype(jnp.int32))


def kernel(x, mem, w_in, hg_lb_logits, hg_norm_w, gd_conv_w, gd_a_log, gd_dt_bias, gd_norm_w, w_out,
           mem_ln_g, mem_ln_b, w_mq, w_mk, w_mv, w_mo, w_group, b_group, w_router, b_router,
           w_gate, w_up, w_down, ln_g, ln_b):
    bsz, seq, d = x.shape
    n = bsz * seq
    lb_all = jnp.cumsum(jax.nn.softmax(hg_lb_logits.astype(F32), axis=0), axis=0)
    lb_all = lb_all - lb_all[0]
    lb_tab = jnp.stack([jnp.log(lb_all), jnp.log1p(-lb_all), 1.0 - lb_all], axis=1)
    sp = np.cumsum((512, 512, 512, 512, 512, 512, 512, GD_HEADS, GD_HEADS, 512))
    seg = lambda a, b: w_in[:, :, (sp[a - 1] if a else 0):sp[b]]
    w_in_r = jnp.concatenate(
        [seg(0, 6), seg(9, 9), seg(7, 8), jnp.zeros((DEPTH, d, HEAD_W - 2 * GD_HEADS), F32)],
        axis=-1).astype(BF16)
    w_out_b = w_out.astype(BF16)
    w_mq_b, w_mo_b = w_mq.astype(BF16), w_mo.astype(BF16)
    w_kv = jnp.stack([w_mk, w_mv], axis=1).astype(BF16)
    w_kv = w_kv.transpose(2, 0, 1, 3).reshape(d, DEPTH * 2 * d)
    pad = ROUTE_W - N_GROUPS - N_EXPERTS
    w_route = jnp.concatenate([w_group, w_router, jnp.zeros((DEPTH, d, pad), F32)], axis=-1).astype(BF16)
    b_route = jnp.concatenate([b_group, b_router, jnp.zeros((DEPTH, pad), F32)], axis=-1)
    b_route = b_route.reshape(DEPTH, 1, ROUTE_W).astype(F32)
    ln_g3 = ln_g.reshape(DEPTH * 3, 1, d)
    ln_b3 = ln_b.reshape(DEPTH * 3, 1, d)

    mem_n = _ln(mem.reshape(-1, d), mem_ln_g, mem_ln_b, ROW_TILE)
    kv_all = _matmul(mem_n, w_kv, BF16, ROW_TILE, 2 * d)

    xt = x.reshape(n, d)
    proj = _matmul(xt, w_in_r[0], F32, ROW_TILE // 2, PROJ_W)
    for l in range(DEPTH):
        y_hg = _hgrn(proj, lb_tab[l], hg_norm_w[l], bsz, seq)
        y_gd = _gdn(proj, gd_conv_w[l], gd_a_log[l], gd_dt_bias[l], gd_norm_w[l], bsz, seq)
        x2, route = _attn_router(y_hg, y_gd, w_out_b, xt, kv_all, w_mq_b, w_mo_b, w_route, b_route,
                                 l, ln_g3, ln_b3, bsz, seq)
        order, bstart, bend, block_expert, n_used = _routing_tables(route, n)
        y = _experts(x2, order, bstart, bend, block_expert, n_used, w_gate, w_up, w_down, l)
        if l + 1 < DEPTH:
            xt, proj = _combine_ln(y, x2, route, l, ln_g3, ln_b3, w_in_r[l + 1])
        else:
            xt = _combine_ln(y, x2, route, l, ln_g3, ln_b3)
    return xt.reshape(bsz, seq, d)
```

```python
import functools
import math

import jax
import jax.numpy as jnp
import numpy as np
from jax import lax
from jax.experimental import pallas as pl
from jax.experimental.pallas import tpu as pltpu

D_MODEL = 1024
DEPTH = 4
HG_HEADS = 4
HG_DK = 128
GD_HEADS = 4
GD_DK = 128
HEAD_W = 128
CONV_K = 4
MEM_HEADS = 4
MEM_DH = D_MODEL // MEM_HEADS
N_GROUPS = 4
EXPERTS_PER_GROUP = 8
N_EXPERTS = N_GROUPS * EXPERTS_PER_GROUP
TOP_K = 2
D_EXPERT = 512
ALPHA = (2.0 * DEPTH) ** 0.25
LN_EPS = 1e-5
RMS_EPS = 1e-6
L2_EPS = 1e-6

COL_HQ, COL_HF, COL_HI, COL_HZ, COL_GQ, COL_GK, COL_GV, COL_GZ, COL_GAB = 0, 4, 8, 12, 16, 20, 24, 28, 32
PROJ_W = 33 * HEAD_W

TIME_BLOCK = 256
GD_CHUNK = 64
ROW_TILE = 512
EXPERT_ROWS = 256
ROUTE_W = 128
LANES = 128
ROW_CHUNKS = D_MODEL // LANES
VMEM_LIMIT_BYTES = 48 * 1024 * 1024

BF16 = jnp.bfloat16
F32 = jnp.float32
NT_DIMS = (((1,), (1,)), ((), ()))
TN_DIMS = (((0,), (0,)), ((), ()))


def _params(n_grid):
    return pltpu.CompilerParams(dimension_semantics=("arbitrary",) * n_grid,
                                vmem_limit_bytes=VMEM_LIMIT_BYTES)


def _mm(a, b):
    return jnp.dot(a.astype(BF16), b.astype(BF16), preferred_element_type=F32)


def _mm_nt(a, b):
    return lax.dot_general(a.astype(BF16), b.astype(BF16), NT_DIMS, preferred_element_type=F32)


def _mm_tn(a, b):
    return lax.dot_general(a.astype(BF16), b.astype(BF16), TN_DIMS, preferred_element_type=F32)


def _cumsum_rows(tri, x):
    p0 = x.astype(BF16)
    r0 = x - p0.astype(F32)
    p1 = r0.astype(BF16)
    p2 = (r0 - p1.astype(F32)).astype(BF16)
    dot = lambda p: jnp.dot(tri, p, preferred_element_type=F32)
    return dot(p0) + dot(p1) + dot(p2)


def _sigmoid(x):
    e = jnp.exp(-jnp.abs(x))
    return jnp.where(x >= 0, 1.0, e) / (1.0 + e)


def _softplus(x):
    return jnp.maximum(x, 0.0) + jnp.log1p(jnp.exp(-jnp.abs(x)))


def _load_token_tiles(ref, n):
    return jnp.concatenate([ref[pl.ds(j, n, stride=ROW_CHUNKS), :] for j in range(ROW_CHUNKS)],
                           axis=1)


def _store_token_tiles(ref, x):
    for j in range(ROW_CHUNKS):
        ref[pl.ds(j, x.shape[0], stride=ROW_CHUNKS), :] = x[:, j * LANES:(j + 1) * LANES]


def _layer_norm(z, g, b):
    mu = jnp.mean(z, axis=-1, keepdims=True)
    zc = z - mu
    var = jnp.mean(zc * zc, axis=-1, keepdims=True)
    return zc * lax.rsqrt(var + LN_EPS) * g + b


def _matmul_kernel(x_ref, w_ref, o_ref):
    o_ref[...] = _mm(x_ref[...], w_ref[...]).astype(o_ref.dtype)


def _matmul(x, w, out_dtype, tm, tn):
    m, k = x.shape
    n = w.shape[1]
    return pl.pallas_call(
        _matmul_kernel,
        grid=(m // tm, n // tn),
        in_specs=[pl.BlockSpec((tm, k), lambda i, j: (i, 0)),
                  pl.BlockSpec((k, tn), lambda i, j: (0, j))],
        out_specs=pl.BlockSpec((tm, tn), lambda i, j: (i, j)),
        out_shape=jax.ShapeDtypeStruct((m, n), out_dtype),
        compiler_params=_params(2),
        name="matmul",
    )(x, w)


def _ln_kernel(x_ref, g_ref, b_ref, o_ref):
    o_ref[...] = _layer_norm(x_ref[...], g_ref[...], b_ref[...])


def _ln(x, g, b, tm):
    n, d = x.shape
    return pl.pallas_call(
        _ln_kernel,
        grid=(n // tm,),
        in_specs=[pl.BlockSpec((tm, d), lambda i: (i, 0)),
                  pl.BlockSpec((1, d), lambda i: (0, 0)),
                  pl.BlockSpec((1, d), lambda i: (0, 0))],
        out_specs=pl.BlockSpec((tm, d), lambda i: (i, 0)),
        out_shape=jax.ShapeDtypeStruct((n, d), F32),
        compiler_params=_params(1),
        name="mem_ln",
    )(x, g.reshape(1, d), b.reshape(1, d))


def _ref_rows(cum, row, h):
    c, w = cum.shape
    if 2 * h <= 8:
        j = row & (2 * h - 1)
        m = cum
        for dl in range(-(h - 1), h + 1):
            if dl != 0:
                m = jnp.where(j - (h - 1) == dl, pltpu.roll(cum, dl % c, axis=0), m)
        return m
    pieces = [jnp.broadcast_to(cum[s + h - 1:s + h, :], (2 * h, w)) for s in range(0, c, 2 * h)]
    return jnp.concatenate(pieces, axis=0) if len(pieces) > 1 else pieces[0]


def _hgrn_kernel(q_ref, f_ref, i_ref, z_ref, lb_ref, nw_ref, tri_ref, msk_ref, o_ref, st_ref):
    @pl.when(pl.program_id(1) == 0)
    def _():
        st_ref[...] = jnp.zeros_like(st_ref)

    c = q_ref.shape[0]
    zf = f_ref[...]
    log_lb, log1m_lb, one_m_lb = lb_ref[0:1, :], lb_ref[1:2, :], lb_ref[2:3, :]
    e = jnp.exp(-jnp.abs(zf))
    log_sig = jnp.minimum(zf, 0.0) - jnp.log1p(e)
    b = log1m_lb + log_sig
    log_f = jnp.maximum(log_lb, b) + jnp.log1p(jnp.exp(-jnp.abs(log_lb - b)))
    k_all = one_m_lb * (jnp.where(zf >= 0, e, 1.0) / (1.0 + e))
    cum_all = _cumsum_rows(tri_ref[...], log_f)

    row = lax.broadcasted_iota(jnp.int32, (c, 1), 0)
    n_lvl = msk_ref.shape[0]
    states = [st_ref[hd] for hd in range(HG_HEADS)]
    ys = []
    for hd in range(HG_HEADS):
        sl = slice(hd * HEAD_W, (hd + 1) * HEAD_W)
        q, k, v, cum = q_ref[:, sl], k_all[:, sl], i_ref[:, sl], cum_all[:, sl]
        scores = msk_ref[n_lvl - 1] * jnp.sum(q * k, axis=-1, keepdims=True)
        for lvl in range(n_lvl):
            h = 1 << lvl
            x = jnp.exp(-jnp.abs(cum - _ref_rows(cum, row, h)))
            upper = (row & (2 * h - 1)) >= h
            sc = _mm_nt(jnp.where(upper, q * x, 0.0), jnp.where(upper, 0.0, k * x))
            scores = scores + (sc * msk_ref[lvl] if 2 * h < c else sc)

        st = states[hd]
        o = _mm_nt(q * jnp.exp(cum), st) + _mm(scores, v)
        cum_last = cum[c - 1:c, :]
        states[hd] = jnp.exp(cum_last) * st + _mm_tn(v, k * jnp.exp(cum_last - cum))
        o = o * lax.rsqrt(jnp.mean(o * o, axis=-1, keepdims=True) + RMS_EPS) * nw_ref[...]
        ys.append((o * _sigmoid(z_ref[:, sl])).astype(o_ref.dtype))
    o_ref[...] = jnp.concatenate(ys, axis=1)
    for hd in range(HG_HEADS):
        st_ref[hd] = states[hd]


def _hgrn(proj, lb_tab, norm_w, bsz, seq):
    nt = seq // TIME_BLOCK
    width = HG_HEADS * HEAD_W
    idx = np.arange(TIME_BLOCK)
    tri = jnp.asarray((idx[:, None] >= idx[None, :]).astype(np.float32), BF16)
    n_lvl = int(math.log2(TIME_BLOCK))
    masks = [(idx[:, None] >> (l + 1)) == (idx[None, :] >> (l + 1)) for l in range(n_lvl - 1)]
    masks = jnp.asarray(np.stack(masks + [idx[:, None] == idx[None, :]]).astype(np.float32))
    col = lambda c0: pl.BlockSpec((TIME_BLOCK, width), lambda b, t: (b * nt + t, c0 // HG_HEADS))
    return pl.pallas_call(
        _hgrn_kernel,
        grid=(bsz, nt),
        in_specs=[col(COL_HQ), col(COL_HF), col(COL_HI), col(COL_HZ),
                  pl.BlockSpec((3, width), lambda b, t: (0, 0)),
                  pl.BlockSpec((1, HEAD_W), lambda b, t: (0, 0)),
                  pl.BlockSpec((TIME_BLOCK, TIME_BLOCK), lambda b, t: (0, 0)),
                  pl.BlockSpec((n_lvl, TIME_BLOCK, TIME_BLOCK), lambda b, t: (0, 0, 0))],
        out_specs=pl.BlockSpec((TIME_BLOCK, width), lambda b, t: (b * nt + t, 0)),
        out_shape=jax.ShapeDtypeStruct((bsz * seq, width), BF16),
        scratch_shapes=[pltpu.VMEM((HG_HEADS, HEAD_W, HG_DK), F32)],
        compiler_params=_params(2),
        name="hgrn2",
    )(proj, proj, proj, proj, lb_tab, norm_w.reshape(1, HEAD_W), tri, masks)


def _conv_silu(x_ref, w_ref, buf_ref, first):
    tb = x_ref.shape[0]

    @pl.when(first)
    def _():
        buf_ref[0:8, :] = jnp.zeros((8, buf_ref.shape[1]), F32)

    @pl.when(jnp.logical_not(first))
    def _():
        buf_ref[0:8, :] = buf_ref[tb:tb + 8, :]

    buf_ref[8:tb + 8, :] = x_ref[...]
    y = w_ref[CONV_K - 1:CONV_K, :] * x_ref[...]
    for j in range(1, CONV_K):
        y = y + w_ref[CONV_K - 1 - j:CONV_K - j, :] * buf_ref[8 - j:8 - j + tb, :]
    return y * _sigmoid(y)


def _gdn_kernel(q_ref, k_ref, v_ref, z_ref, ab_ref, wq_ref, wk_ref, wv_ref, alog_ref, dtb_ref,
                nw_ref, tri_ref, msk_ref, o_ref, s_ref, bq_ref, bk_ref, bv_ref):
    first = pl.program_id(1) == 0

    @pl.when(first)
    def _():
        s_ref[...] = jnp.zeros_like(s_ref)

    tb = q_ref.shape[0]
    cq_all = _conv_silu(q_ref, wq_ref, bq_ref, first)
    ck_all = _conv_silu(k_ref, wk_ref, bk_ref, first)
    v_all = _conv_silu(v_ref, wv_ref, bv_ref, first)

    lane = lax.broadcasted_iota(jnp.int32, (1, HEAD_W), 1)
    ab = ab_ref[...]
    g_all = jnp.where(lane < GD_HEADS, -jnp.exp(alog_ref[...]) * _softplus(ab + dtb_ref[...]), 0.0)
    beta_all = _sigmoid(ab)
    lc_all = _cumsum_rows(tri_ref[...], g_all)
    lc_all_t = jnp.transpose(lc_all)
    n_sq = int(math.log2(GD_CHUNK)) - 1

    heads = range(GD_HEADS)
    sls = [slice(hd * HEAD_W, (hd + 1) * HEAD_W) for hd in heads]
    s = [s_ref[hd] for hd in heads]
    q, k, kb, beta, lc_col, qk, t, pw = [], [], [], [], [], [], [], []
    for hd in heads:
        cq, ck = cq_all[:, sls[hd]], ck_all[:, sls[hd]]
        q.append(cq * lax.rsqrt(jnp.sum(cq * cq, axis=-1, keepdims=True) + L2_EPS) * (GD_DK ** -0.5))
        k.append(ck * lax.rsqrt(jnp.sum(ck * ck, axis=-1, keepdims=True) + L2_EPS))
        beta.append(beta_all[:, GD_HEADS + hd:GD_HEADS + hd + 1])
        lc_col.append(lc_all[:, hd:hd + 1])
        decay = jnp.exp(lc_col[hd] - lc_all_t[hd:hd + 1, :] + msk_ref[0])
        kb.append(k[hd] * beta[hd])
        p = _mm_nt(jnp.concatenate([q[hd], kb[hd]], axis=0), k[hd])
        qk.append(p[:tb] * decay)
        a = p[tb:] * decay * msk_ref[1]
        t.append(msk_ref[2] - a)
        pw.append(a)
    for _ in range(n_sq):
        for hd in heads:
            pw[hd] = _mm(pw[hd], pw[hd])
            t[hd] = t[hd] + _mm(t[hd], pw[hd])
    u, w, qe = [], [], []
    for hd in heads:
        e_lc = jnp.exp(lc_col[hd])
        uw = _mm(t[hd], jnp.concatenate([v_all[:, sls[hd]] * beta[hd], kb[hd] * e_lc], axis=1))
        u.append(uw[:, :HEAD_W])
        w.append(uw[:, HEAD_W:])
        qe.append(q[hd] * e_lc)
    outs = [[] for _ in heads]
    for c0 in range(0, tb, GD_CHUNK):
        c1 = c0 + GD_CHUNK
        for hd in heads:
            r = _mm(jnp.concatenate([w[hd][c0:c1], qe[hd][c0:c1]], axis=0), s[hd])
            v_new = u[hd][c0:c1] - r[:GD_CHUNK]
            pads = ([jnp.zeros((c0, HEAD_W), F32)] if c0 else []) + [v_new] + (
                [jnp.zeros((tb - c1, HEAD_W), F32)] if c1 < tb else [])
            outs[hd].append(r[GD_CHUNK:] + _mm(qk[hd][c0:c1, :], jnp.concatenate(pads, axis=0)))
            lc_last = lc_col[hd][c1 - 1:c1, :]
            kd = k[hd][c0:c1] * jnp.exp(lc_last - lc_col[hd][c0:c1])
            s[hd] = jnp.exp(lc_last) * s[hd] + _mm_tn(kd, v_new)
    ys = []
    for hd in heads:
        o = jnp.concatenate(outs[hd], axis=0)
        o = o * lax.rsqrt(jnp.mean(o * o, axis=-1, keepdims=True) + RMS_EPS) * nw_ref[...]
        zg = z_ref[:, sls[hd]]
        ys.append((o * (zg * _sigmoid(zg))).astype(o_ref.dtype))
    o_ref[...] = jnp.concatenate(ys, axis=1)
    for hd in heads:
        s_ref[hd] = s[hd]


def _gdn(proj, conv_w, a_log, dt_bias, norm_w, bsz, seq):
    nt = seq // TIME_BLOCK
    idx = np.arange(TIME_BLOCK)
    tri = (idx[:, None] >= idx[None, :]) & (idx[:, None] // GD_CHUNK == idx[None, :] // GD_CHUNK)
    eye = idx[:, None] == idx[None, :]
    masks = jnp.asarray(np.stack([np.where(tri, 0.0, -np.inf), 1.0 - eye, eye]).astype(np.float32))
    tri = jnp.asarray(tri.astype(np.float32), BF16)
    width = GD_HEADS * HEAD_W
    lane_pad = lambda p: jnp.concatenate([p.astype(F32), jnp.zeros((HEAD_W - GD_HEADS,), F32)]
                                         ).reshape(1, HEAD_W)
    col = lambda c0: pl.BlockSpec((TIME_BLOCK, width), lambda b, t: (b * nt + t, c0 // GD_HEADS))
    cw = lambda c0: pl.BlockSpec((CONV_K, width), lambda b, t: (0, c0))
    row = lambda: pl.BlockSpec((1, HEAD_W), lambda b, t: (0, 0))
    return pl.pallas_call(
        _gdn_kernel,
        grid=(bsz, nt),
        in_specs=[col(COL_GQ), col(COL_GK), col(COL_GV), col(COL_GZ),
                  pl.BlockSpec((TIME_BLOCK, HEAD_W), lambda b, t: (b * nt + t, COL_GAB)),
                  cw(0), cw(1), cw(2), row(), row(), row(),
                  pl.BlockSpec((TIME_BLOCK, TIME_BLOCK), lambda b, t: (0, 0)),
                  pl.BlockSpec((3, TIME_BLOCK, TIME_BLOCK), lambda b, t: (0, 0, 0))],
        out_specs=pl.BlockSpec((TIME_BLOCK, width), lambda b, t: (b * nt + t, 0)),
        out_shape=jax.ShapeDtypeStruct((bsz * seq, width), BF16),
        scratch_shapes=[pltpu.VMEM((GD_HEADS, GD_DK, HEAD_W), F32)] + [
            pltpu.VMEM((TIME_BLOCK + 8, width), F32) for _ in range(3)],
        compiler_params=_params(2),
        name="gated_deltanet",
    )(proj, proj, proj, proj, proj, conv_w, conv_w, conv_w, lane_pad(a_log), lane_pad(dt_bias),
      norm_w.reshape(1, HEAD_W), tri, masks)


def _attn_router_kernel(yh_ref, yg_ref, wh_ref, wg_ref, x0_ref, g0_ref, b0_ref, wq_ref, k_ref, v_ref,
                        wo_ref, g_ref, b_ref, wr_ref, br_ref, o_ref, route_ref):
    half = x0_ref.shape[0] // 2
    parts = [slice(0, half), slice(half, 2 * half)]
    x = []
    for rs in parts:
        mix = (jnp.dot(yh_ref[rs, :], wh_ref[...], preferred_element_type=F32)
               + jnp.dot(yg_ref[rs, :], wg_ref[...], preferred_element_type=F32))
        x.append(_layer_norm(ALPHA * x0_ref[rs, :] + mix, g0_ref[...], b0_ref[...]))
    q = [_mm(xi, wq_ref[...]).astype(BF16) for xi in x]
    heads = [[] for _ in parts]
    for h in range(MEM_HEADS):
        sl = slice(h * MEM_DH, (h + 1) * MEM_DH)
        for j in range(len(parts)):
            s = lax.dot_general(q[j][:, sl], k_ref[:, sl], NT_DIMS, preferred_element_type=F32)
            s = s * (MEM_DH ** -0.5)
            p = jnp.exp(s - jnp.max(s, axis=-1, keepdims=True))
            p = p / jnp.sum(p, axis=-1, keepdims=True)
            heads[j].append(_mm(p, v_ref[:, sl]))
    x2 = [_layer_norm(ALPHA * x[j] + _mm(jnp.concatenate(heads[j], axis=1), wo_ref[...]),
                      g_ref[...], b_ref[...]) for j in range(len(parts))]
    for j, rs in enumerate(parts):
        _store_token_tiles(o_ref.at[pl.ds(rs.start * ROW_CHUNKS, half * ROW_CHUNKS), :], x2[j])
        route_ref[rs, :] = _route_select(_mm(x2[j], wr_ref[...]) + br_ref[...])


def _route_select(logits):
    lane = lax.broadcasted_iota(jnp.int32, (1, ROUTE_W), 1).astype(F32)
    neg = -jnp.inf
    first_at = lambda val, top: jnp.min(jnp.where(val == top, lane, float(ROUTE_W)), axis=-1,
                                        keepdims=True)
    gl = jnp.where(lane < N_GROUPS, logits, neg)
    g_max = jnp.max(gl, axis=-1, keepdims=True)
    g_sel = first_at(gl, g_max)
    g_gate = 1.0 / jnp.sum(jnp.exp(gl - g_max), axis=-1, keepdims=True)
    lo = N_GROUPS + EXPERTS_PER_GROUP * g_sel
    el = jnp.where((lane >= lo) & (lane < lo + EXPERTS_PER_GROUP), logits, neg)
    t1 = jnp.max(el, axis=-1, keepdims=True)
    i1 = first_at(el, t1)
    el2 = jnp.where(lane == i1, neg, el)
    t2 = jnp.max(el2, axis=-1, keepdims=True)
    i2 = first_at(el2, t2)
    e2 = jnp.exp(t2 - t1)
    w1 = g_gate / (1.0 + e2)
    w2 = g_gate * e2 / (1.0 + e2)
    return jnp.where(lane == 0, i1 - N_GROUPS,
                     jnp.where(lane == 1, i2 - N_GROUPS,
                               jnp.where(lane == 2, w1, jnp.where(lane == 3, w2, 0.0))))


def _attn_router(y_hg, y_gd, w_out, x, kv_all, w_mq, w_mo, w_route, b_route, layer, ln_g, ln_b,
                 bsz, seq):
    n, d = x.shape
    hw = y_hg.shape[1]
    tq = ROW_TILE
    nq = seq // tq
    mlen = kv_all.shape[0] // bsz
    wspec = lambda: pl.BlockSpec((None, d, d), lambda b, i: (layer, 0, 0))
    lnspec = lambda j: pl.BlockSpec((None, 1, d), lambda b, i: (layer * 3 + j, 0, 0))
    yspec = lambda: pl.BlockSpec((tq, hw), lambda b, i: (b * nq + i, 0))
    return pl.pallas_call(
        _attn_router_kernel,
        grid=(bsz, nq),
        in_specs=[yspec(), yspec(),
                  pl.BlockSpec((None, hw, d), lambda b, i: (layer, 0, 0)),
                  pl.BlockSpec((None, hw, d), lambda b, i: (layer, 1, 0)),
                  pl.BlockSpec((tq, d), lambda b, i: (b * nq + i, 0)),
                  lnspec(0), lnspec(0),
                  wspec(),
                  pl.BlockSpec((mlen, d), lambda b, i: (b, 2 * layer)),
                  pl.BlockSpec((mlen, d), lambda b, i: (b, 2 * layer + 1)),
                  wspec(), lnspec(1), lnspec(1),
                  pl.BlockSpec((None, d, ROUTE_W), lambda b, i: (layer, 0, 0)),
                  pl.BlockSpec((None, 1, ROUTE_W), lambda b, i: (layer, 0, 0))],
        out_specs=[pl.BlockSpec((tq * ROW_CHUNKS, LANES), lambda b, i: (b * nq + i, 0)),
                   pl.BlockSpec((tq, ROUTE_W), lambda b, i: (b * nq + i, 0))],
        out_shape=[jax.ShapeDtypeStruct((n * ROW_CHUNKS, LANES), F32),
                   jax.ShapeDtypeStruct((n, ROUTE_W), F32)],
        compiler_params=_params(2),
        name="attn_router",
    )(y_hg, y_gd, w_out, w_out, x, ln_g, ln_b, w_mq, kv_all, kv_all, w_mo, ln_g, ln_b, w_route,
      b_route)


def _expert_kernel(order_ref, bstart_ref, bend_ref, be_ref, nu_ref, x_hbm, zeros_hbm, wg_ref, wu_ref,
                   wd_ref, y_hbm, rows0, rows1, yv0, yv1, gsem, ssem, zsem, wg_bf, wu_bf, wd_bf):
    b = pl.program_id(0)
    nu = nu_ref[0]
    m = order_ref.shape[0] - EXPERT_ROWS
    n_tok = m // TOP_K
    k_bits = int(math.log2(TOP_K))
    rows, yv = (rows0, rows1), (yv0, yv1)
    block_rows = EXPERT_ROWS * ROW_CHUNKS

    def tile(r):
        if isinstance(r, int):
            return pl.ds(r * ROW_CHUNKS, ROW_CHUNKS)
        return pl.ds(pl.multiple_of(r * ROW_CHUNKS, ROW_CHUNKS), ROW_CHUNKS)

    def gather_row(base, i, s):
        a = order_ref[base + i]
        pltpu.make_async_copy(x_hbm.at[tile(a >> k_bits), :], rows[s].at[tile(i), :],
                              gsem.at[s]).start()

    def scatter_row(base, n_valid, i, s):
        a = order_ref[base + i]
        row = jnp.where(i < n_valid, (a & (TOP_K - 1)) * n_tok + (a >> k_bits), m + i)
        pltpu.make_async_copy(yv[s].at[tile(i), :], y_hbm.at[tile(row), :], ssem.at[s]).start()

    def wait_gather(s):
        pltpu.make_async_copy(x_hbm.at[pl.ds(0, block_rows), :], rows[s], gsem.at[s]).wait()

    def wait_scatter(s):
        pltpu.make_async_copy(yv[s], y_hbm.at[pl.ds(0, block_rows), :], ssem.at[s]).wait()

    @pl.when(b == 0)
    def _():
        zc = pltpu.make_async_copy(zeros_hbm, y_hbm.at[pl.ds(m * ROW_CHUNKS, block_rows), :], zsem)
        zc.start()
        zc.wait()
        yv1[...] = jnp.zeros_like(yv1)
        base = bstart_ref[0]

        def body(i, carry):
            gather_row(base, i, 0)
            return carry
        lax.fori_loop(0, EXPERT_ROWS, body, 0, unroll=8)

    @pl.when(jnp.logical_or(b == 0, be_ref[b] != be_ref[jnp.maximum(b - 1, 0)]))
    def _():
        wg_bf[...] = wg_ref[...].astype(BF16)
        wu_bf[...] = wu_ref[...].astype(BF16)
        wd_bf[...] = wd_ref[...].astype(BF16)

    def step(s):
        o = 1 - s
        wait_gather(s)
        nxt_base = bstart_ref[jnp.minimum(b + 1, nu - 1)]
        for i in range(EXPERT_ROWS):
            gather_row(nxt_base, i, o)
        prev = jnp.maximum(b - 1, 0)
        prev_base = bstart_ref[prev]
        prev_valid = jnp.where(b == 0, 0, bend_ref[prev] - prev_base)
        for i in range(EXPERT_ROWS):
            scatter_row(prev_base, prev_valid, i, o)

        xb = _load_token_tiles(rows[s], EXPERT_ROWS).astype(BF16)
        hg = jnp.dot(xb, wg_bf[...], preferred_element_type=F32)
        hu = jnp.dot(xb, wu_bf[...], preferred_element_type=F32)
        hid = hg * _sigmoid(hg) * hu
        y = jnp.dot(hid.astype(BF16), wd_bf[...], preferred_element_type=F32)

        @pl.when(b >= 1)
        def _():
            wait_scatter(s)

        _store_token_tiles(yv[s], y)

        @pl.when(b == nu - 1)
        def _():
            base = bstart_ref[b]
            n_valid = bend_ref[b] - base

            def body(i, carry):
                scatter_row(base, n_valid, i, s)
                return carry
            lax.fori_loop(0, EXPERT_ROWS, body, 0, unroll=8)
            wait_gather(o)
            wait_scatter(o)
            wait_scatter(s)

    for parity in range(2):
        @pl.when(jnp.logical_and(b < nu, b % 2 == parity))
        def _():
            step(parity)


def _experts(x, order, bstart, bend, block_expert, n_used, w_gate, w_up, w_down, layer):
    d = D_MODEL
    m = order.shape[0] - EXPERT_ROWS
    block_rows = EXPERT_ROWS * ROW_CHUNKS
    nb = m // EXPERT_ROWS + N_EXPERTS
    any_spec = pl.BlockSpec(memory_space=pl.ANY)
    wspec = lambda r, c: pl.BlockSpec((None, None, r, c),
                                      lambda b, o, bs, bn, be, nu: (layer, be[b], 0, 0))
    grid_spec = pltpu.PrefetchScalarGridSpec(
        num_scalar_prefetch=5, grid=(nb,),
        in_specs=[any_spec, any_spec, wspec(d, D_EXPERT), wspec(d, D_EXPERT), wspec(D_EXPERT, d)],
        out_specs=any_spec,
        scratch_shapes=[pltpu.VMEM((block_rows, LANES), F32) for _ in range(4)] + [
                        pltpu.SemaphoreType.DMA((2,)), pltpu.SemaphoreType.DMA((2,)),
                        pltpu.SemaphoreType.DMA(()),
                        pltpu.VMEM((d, D_EXPERT), BF16), pltpu.VMEM((d, D_EXPERT), BF16),
                        pltpu.VMEM((D_EXPERT, d), BF16)])
    return pl.pallas_call(
        _expert_kernel,
        grid_spec=grid_spec,
        out_shape=jax.ShapeDtypeStruct((m * ROW_CHUNKS + block_rows, LANES), F32),
        compiler_params=_params(1),
        name="moe_experts",
    )(order, bstart, bend, block_expert, n_used, x, jnp.zeros((block_rows, LANES), F32),
      w_gate, w_up, w_down)


def _combine_ln_kernel(y0_ref, y1_ref, x_ref, route_ref, g_ref, b_ref, *rest):
    route = route_ref[...]
    tm = route.shape[0]
    ff = (route[:, 2:3] * _load_token_tiles(y0_ref, tm)
          + route[:, 3:4] * _load_token_tiles(y1_ref, tm))
    xn = _layer_norm(ALPHA * _load_token_tiles(x_ref, tm) + ff, g_ref[...], b_ref[...])
    if len(rest) == 1:
        (o_ref,) = rest
    else:
        w_ref, o_ref, proj_ref = rest
        proj_ref[...] = _mm(xn, w_ref[...])
    o_ref[...] = xn


def _combine_ln(y, x, route, layer, ln_g, ln_b, w_next=None):
    n, d = route.shape[0], D_MODEL
    tm = ROW_TILE if w_next is None else ROW_TILE // 2
    lnspec = lambda: pl.BlockSpec((None, 1, d), lambda i: (layer * 3 + 2, 0, 0))
    tiles = lambda off: pl.BlockSpec((tm * ROW_CHUNKS, LANES), lambda i: (off + i, 0))
    in_specs = [tiles(0), tiles(n // tm), tiles(0), pl.BlockSpec((tm, ROUTE_W), lambda i: (i, 0)),
                lnspec(), lnspec()]
    out_specs = pl.BlockSpec((tm, d), lambda i: (i, 0))
    out_shape = jax.ShapeDtypeStruct((n, d), F32)
    args = (y, y, x, route, ln_g, ln_b)
    if w_next is not None:
        pw = w_next.shape[1]
        in_specs.append(pl.BlockSpec((d, pw), lambda i: (0, 0)))
        out_specs = [out_specs, pl.BlockSpec((tm, pw), lambda i: (i, 0))]
        out_shape = [out_shape, jax.ShapeDtypeStruct((n, pw), F32)]
        args += (w_next,)
    return pl.pallas_call(
        _combine_ln_kernel,
        grid=(n // tm,),
        in_specs=in_specs,
        out_specs=out_specs,
        out_shape=out_shape,
        compiler_params=_params(1),
        name="moe_combine_ln",
    )(*args)


def _routing_tables(route, n_tok):
    m = n_tok * TOP_K
    assert m * N_EXPERTS < 2 ** 31
    flat_e = route[:, 0:TOP_K].astype(jnp.int32).reshape(m)
    keys = jnp.sort(flat_e * m + jnp.arange(m, dtype=jnp.int32))
    order = jnp.concatenate([keys % m, jnp.zeros((EXPERT_ROWS,), jnp.int32)])
    bounds = jnp.searchsorted(keys, jnp.arange(N_EXPERTS + 1, dtype=jnp.int32) * m, side='left',
                              method='compare_all')
    starts = bounds[:-1].astype(jnp.int32)
    counts = (bounds[1:] - bounds[:-1]).astype(jnp.int32)
    pcounts = (counts + EXPERT_ROWS - 1) // EXPERT_ROWS * EXPERT_ROWS
    pends = jnp.cumsum(pcounts)
    n_blocks = m // EXPERT_ROWS + N_EXPERTS
    n_used = pends[-1] // EXPERT_ROWS
    blk = jnp.minimum(jnp.arange(n_blocks, dtype=jnp.int32), n_used - 1)
    block_expert = jnp.minimum(jnp.searchsorted(pends, blk * EXPERT_ROWS, side='right',
                                                method='compare_all'),
                               N_EXPERTS - 1).astype(jnp.int32)
    bstart = starts[block_expert] + blk * EXPERT_ROWS - (pends - pcounts)[block_expert]
    bend = (starts + counts)[block_expert]
    return (order, bstart.astype(jnp.int32), bend.astype(jnp.int32), block_expert,
            n_used.reshape(1).astype(jnp.int32))


def kernel(x, mem, w_in, hg_lb_logits, hg_norm_w, gd_conv_w, gd_a_log, gd_dt_bias, gd_norm_w, w_out,
           mem_ln_g, mem_ln_b, w_mq, w_mk, w_mv, w_mo, w_group, b_group, w_router, b_router,
           w_gate, w_up, w_down, ln_g, ln_b):
    bsz, seq, d = x.shape
    n = bsz * seq
    lb_all = jnp.cumsum(jax.nn.softmax(hg_lb_logits.astype(F32), axis=0), axis=0)
    lb_all = lb_all - lb_all[0]
    lb_tab = jnp.stack([jnp.log(lb_all), jnp.log1p(-lb_all), 1.0 - lb_all], axis=1)
    sp = np.cumsum((512, 512, 512, 512, 512, 512, 512, GD_HEADS, GD_HEADS, 512))
    seg = lambda a, b: w_in[:, :, (sp[a - 1] if a else 0):sp[b]]
    w_in_r = jnp.concatenate(
        [seg(0, 6), seg(9, 9), seg(7, 8), jnp.zeros((DEPTH, d, HEAD_W - 2 * GD_HEADS), F32)],
        axis=-1).astype(BF16)
    w_out_b = w_out.astype(BF16)
    w_mq_b, w_mo_b = w_mq.astype(BF16), w_mo.astype(BF16)
    w_kv = jnp.stack([w_mk, w_mv], axis=1).astype(BF16)
    w_kv = w_kv.transpose(2, 0, 1, 3).reshape(d, DEPTH * 2 * d)
    pad = ROUTE_W - N_GROUPS - N_EXPERTS
    w_route = jnp.concatenate([w_group, w_router, jnp.zeros((DEPTH, d, pad), F32)], axis=-1).astype(BF16)
    b_route = jnp.concatenate([b_group, b_router, jnp.zeros((DEPTH, pad), F32)], axis=-1)
    b_route = b_route.reshape(DEPTH, 1, ROUTE_W).astype(F32)
    ln_g3 = ln_g.reshape(DEPTH * 3, 1, d)
    ln_b3 = ln_b.reshape(DEPTH * 3, 1, d)

    mem_n = _ln(mem.reshape(-1, d), mem_ln_g, mem_ln_b, ROW_TILE)
    kv_all = _matmul(mem_n, w_kv, BF16, ROW_TILE, 2 * d)

    xt = x.reshape(n, d)
    proj = _matmul(xt, w_in_r[0], F32, ROW_TILE // 2, PROJ_W)
    for l in range(DEPTH):
        y_hg = _hgrn(proj, lb_tab[l], hg_norm_w[l], bsz, seq)
        y_gd = _gdn(proj, gd_conv_w[l], gd_a_log[l], gd_dt_bias[l], gd_norm_w[l], bsz, seq)
        x2, route = _attn_router(y_hg, y_gd, w_out_b, xt, kv_all, w_mq_b, w_mo_b, w_route, b_route,
                                 l, ln_g3, ln_b3, bsz, seq)
        order, bstart, bend, block_expert, n_used = _routing_tables(route, n)
        y = _experts(x2, order, bstart, bend, block_expert, n_used, w_gate, w_up, w_down, l)
        if l + 1 < DEPTH:
            xt, proj = _combine_ln(y, x2, route, l, ln_g3, ln_b3, w_in_r[l + 1])
        else:
            xt = _combine_ln(y, x2, route, l, ln_g3, ln_b3)
    return xt.reshape(bsz, seq, d)
```

```python
import functools
import math

import jax
import jax.numpy as jnp
import numpy as np
from jax import lax
from jax.experimental import pallas as pl
from jax.experimental.pallas import tpu as pltpu

D_MODEL = 1024
DEPTH = 4
HG_HEADS = 4
HG_DK = 128
GD_HEADS = 4
GD_DK = 128
HEAD_W = 128
CONV_K = 4
MEM_HEADS = 4
MEM_DH = D_MODEL // MEM_HEADS
N_GROUPS = 4
EXPERTS_PER_GROUP = 8
N_EXPERTS = N_GROUPS * EXPERTS_PER_GROUP
TOP_K = 2
D_EXPERT = 512
ALPHA = (2.0 * DEPTH) ** 0.25
LN_EPS = 1e-5
RMS_EPS = 1e-6
L2_EPS = 1e-6

COL_HQ, COL_HF, COL_HI, COL_HZ, COL_GQ, COL_GK, COL_GV, COL_GZ, COL_GAB = 0, 4, 8, 12, 16, 20, 24, 28, 32
PROJ_W = 33 * HEAD_W

TIME_BLOCK = 256
GD_CHUNK = 64
ROW_TILE = 512
EXPERT_ROWS = 256
ROUTE_W = 128
LANES = 128
ROW_CHUNKS = D_MODEL // LANES
VMEM_LIMIT_BYTES = 56 * 1024 * 1024

BF16 = jnp.bfloat16
F32 = jnp.float32
NT_DIMS = (((1,), (1,)), ((), ()))
TN_DIMS = (((0,), (0,)), ((), ()))


def _params(n_grid):
    return pltpu.CompilerParams(dimension_semantics=("arbitrary",) * n_grid,
                                vmem_limit_bytes=VMEM_LIMIT_BYTES)


def _mm(a, b):
    return jnp.dot(a.astype(BF16), b.astype(BF16), preferred_element_type=F32)


def _mm_nt(a, b):
    return lax.dot_general(a.astype(BF16), b.astype(BF16), NT_DIMS, preferred_element_type=F32)


def _mm_tn(a, b):
    return lax.dot_general(a.astype(BF16), b.astype(BF16), TN_DIMS, preferred_element_type=F32)


def _cumsum_rows(tri, x):
    p0 = x.astype(BF16)
    r0 = x - p0.astype(F32)
    p1 = r0.astype(BF16)
    p2 = (r0 - p1.astype(F32)).astype(BF16)
    dot = lambda p: jnp.dot(tri, p, preferred_element_type=F32)
    return dot(p0) + dot(p1) + dot(p2)


def _sigmoid(x):
    e = jnp.exp(-jnp.abs(x))
    return jnp.where(x >= 0, 1.0, e) / (1.0 + e)


def _softplus(x):
    return jnp.maximum(x, 0.0) + jnp.log1p(jnp.exp(-jnp.abs(x)))


def _load_token_tiles(ref, n):
    return jnp.concatenate([ref[pl.ds(j, n, stride=ROW_CHUNKS), :] for j in range(ROW_CHUNKS)],
                           axis=1)


def _store_token_tiles(ref, x):
    for j in range(ROW_CHUNKS):
        ref[pl.ds(j, x.shape[0], stride=ROW_CHUNKS), :] = x[:, j * LANES:(j + 1) * LANES]


def _layer_norm(z, g, b):
    mu = jnp.mean(z, axis=-1, keepdims=True)
    zc = z - mu
    var = jnp.mean(zc * zc, axis=-1, keepdims=True)
    return zc * lax.rsqrt(var + LN_EPS) * g + b


def _matmul_kernel(x_ref, w_ref, o_ref):
    o_ref[...] = _mm(x_ref[...], w_ref[...]).astype(o_ref.dtype)


def _matmul(x, w, out_dtype, tm, tn):
    m, k = x.shape
    n = w.shape[1]
    return pl.pallas_call(
        _matmul_kernel,
        grid=(m // tm, n // tn),
        in_specs=[pl.BlockSpec((tm, k), lambda i, j: (i, 0)),
                  pl.BlockSpec((k, tn), lambda i, j: (0, j))],
        out_specs=pl.BlockSpec((tm, tn), lambda i, j: (i, j)),
        out_shape=jax.ShapeDtypeStruct((m, n), out_dtype),
        compiler_params=_params(2),
        name="matmul",
    )(x, w)


def _ln_kernel(x_ref, g_ref, b_ref, o_ref):
    o_ref[...] = _layer_norm(x_ref[...], g_ref[...], b_ref[...])


def _ln(x, g, b, tm):
    n, d = x.shape
    return pl.pallas_call(
        _ln_kernel,
        grid=(n // tm,),
        in_specs=[pl.BlockSpec((tm, d), lambda i: (i, 0)),
                  pl.BlockSpec((1, d), lambda i: (0, 0)),
                  pl.BlockSpec((1, d), lambda i: (0, 0))],
        out_specs=pl.BlockSpec((tm, d), lambda i: (i, 0)),
        out_shape=jax.ShapeDtypeStruct((n, d), F32),
        compiler_params=_params(1),
        name="mem_ln",
    )(x, g.reshape(1, d), b.reshape(1, d))


def _ref_rows(cum, row, h):
    c, w = cum.shape
    if 2 * h <= 8:
        j = row & (2 * h - 1)
        m = cum
        for dl in range(-(h - 1), h + 1):
            if dl != 0:
                m = jnp.where(j - (h - 1) == dl, pltpu.roll(cum, dl % c, axis=0), m)
        return m
    pieces = [jnp.broadcast_to(cum[s + h - 1:s + h, :], (2 * h, w)) for s in range(0, c, 2 * h)]
    return jnp.concatenate(pieces, axis=0) if len(pieces) > 1 else pieces[0]


def _hgrn_kernel(q_ref, f_ref, i_ref, z_ref, lb_ref, nw_ref, tri_ref, msk_ref, o_ref, st_ref):
    @pl.when(pl.program_id(1) == 0)
    def _():
        st_ref[...] = jnp.zeros_like(st_ref)

    c = q_ref.shape[0]
    zf = f_ref[...]
    log_lb, log1m_lb, one_m_lb = lb_ref[0:1, :], lb_ref[1:2, :], lb_ref[2:3, :]
    e = jnp.exp(-jnp.abs(zf))
    log_sig = jnp.minimum(zf, 0.0) - jnp.log1p(e)
    b = log1m_lb + log_sig
    log_f = jnp.maximum(log_lb, b) + jnp.log1p(jnp.exp(-jnp.abs(log_lb - b)))
    k_all = one_m_lb * (jnp.where(zf >= 0, e, 1.0) / (1.0 + e))
    cum_all = _cumsum_rows(tri_ref[...], log_f)

    row = lax.broadcasted_iota(jnp.int32, (c, 1), 0)
    n_lvl = msk_ref.shape[0]
    states = [st_ref[hd] for hd in range(HG_HEADS)]
    ys = []
    for hd in range(HG_HEADS):
        sl = slice(hd * HEAD_W, (hd + 1) * HEAD_W)
        q, k, v, cum = q_ref[:, sl], k_all[:, sl], i_ref[:, sl], cum_all[:, sl]
        scores = msk_ref[n_lvl - 1] * jnp.sum(q * k, axis=-1, keepdims=True)
        for lvl in range(n_lvl):
            h = 1 << lvl
            x = jnp.exp(-jnp.abs(cum - _ref_rows(cum, row, h)))
            upper = (row & (2 * h - 1)) >= h
            sc = _mm_nt(jnp.where(upper, q * x, 0.0), jnp.where(upper, 0.0, k * x))
            scores = scores + (sc * msk_ref[lvl] if 2 * h < c else sc)

        st = states[hd]
        o = _mm_nt(q * jnp.exp(cum), st) + _mm(scores, v)
        cum_last = cum[c - 1:c, :]
        states[hd] = jnp.exp(cum_last) * st + _mm_tn(v, k * jnp.exp(cum_last - cum))
        o = o * lax.rsqrt(jnp.mean(o * o, axis=-1, keepdims=True) + RMS_EPS) * nw_ref[...]
        ys.append((o * _sigmoid(z_ref[:, sl])).astype(o_ref.dtype))
    o_ref[...] = jnp.concatenate(ys, axis=1)
    for hd in range(HG_HEADS):
        st_ref[hd] = states[hd]


def _hgrn(proj, lb_tab, norm_w, bsz, seq):
    nt = seq // TIME_BLOCK
    width = HG_HEADS * HEAD_W
    idx = np.arange(TIME_BLOCK)
    tri = jnp.asarray((idx[:, None] >= idx[None, :]).astype(np.float32), BF16)
    n_lvl = int(math.log2(TIME_BLOCK))
    masks = [(idx[:, None] >> (l + 1)) == (idx[None, :] >> (l + 1)) for l in range(n_lvl - 1)]
    masks = jnp.asarray(np.stack(masks + [idx[:, None] == idx[None, :]]).astype(np.float32))
    col = lambda c0: pl.BlockSpec((TIME_BLOCK, width), lambda b, t: (b * nt + t, c0 // HG_HEADS))
    return pl.pallas_call(
        _hgrn_kernel,
        grid=(bsz, nt),
        in_specs=[col(COL_HQ), col(COL_HF), col(COL_HI), col(COL_HZ),
                  pl.BlockSpec((3, width), lambda b, t: (0, 0)),
                  pl.BlockSpec((1, HEAD_W), lambda b, t: (0, 0)),
                  pl.BlockSpec((TIME_BLOCK, TIME_BLOCK), lambda b, t: (0, 0)),
                  pl.BlockSpec((n_lvl, TIME_BLOCK, TIME_BLOCK), lambda b, t: (0, 0, 0))],
        out_specs=pl.BlockSpec((TIME_BLOCK, width), lambda b, t: (b * nt + t, 0)),
        out_shape=jax.ShapeDtypeStruct((bsz * seq, width), BF16),
        scratch_shapes=[pltpu.VMEM((HG_HEADS, HEAD_W, HG_DK), F32)],
        compiler_params=_params(2),
        name="hgrn2",
    )(proj, proj, proj, proj, lb_tab, norm_w.reshape(1, HEAD_W), tri, masks)


def _conv_silu(x_ref, w_ref, buf_ref, first):
    tb = x_ref.shape[0]

    @pl.when(first)
    def _():
        buf_ref[0:8, :] = jnp.zeros((8, buf_ref.shape[1]), F32)

    @pl.when(jnp.logical_not(first))
    def _():
        buf_ref[0:8, :] = buf_ref[tb:tb + 8, :]

    buf_ref[8:tb + 8, :] = x_ref[...]
    y = w_ref[CONV_K - 1:CONV_K, :] * x_ref[...]
    for j in range(1, CONV_K):
        y = y + w_ref[CONV_K - 1 - j:CONV_K - j, :] * buf_ref[8 - j:8 - j + tb, :]
    return y * _sigmoid(y)


def _gdn_kernel(q_ref, k_ref, v_ref, z_ref, ab_ref, wq_ref, wk_ref, wv_ref, alog_ref, dtb_ref,
                nw_ref, tri_ref, msk_ref, o_ref, s_ref, bq_ref, bk_ref, bv_ref):
    first = pl.program_id(1) == 0

    @pl.when(first)
    def _():
        s_ref[...] = jnp.zeros_like(s_ref)

    tb = q_ref.shape[0]
    cq_all = _conv_silu(q_ref, wq_ref, bq_ref, first)
    ck_all = _conv_silu(k_ref, wk_ref, bk_ref, first)
    v_all = _conv_silu(v_ref, wv_ref, bv_ref, first)

    lane = lax.broadcasted_iota(jnp.int32, (1, HEAD_W), 1)
    ab = ab_ref[...]
    g_all = jnp.where(lane < GD_HEADS, -jnp.exp(alog_ref[...]) * _softplus(ab + dtb_ref[...]), 0.0)
    beta_all = _sigmoid(ab)
    lc_all = _cumsum_rows(tri_ref[...], g_all)
    lc_all_t = jnp.transpose(lc_all)
    n_sq = int(math.log2(GD_CHUNK)) - 1

    heads = range(GD_HEADS)
    sls = [slice(hd * HEAD_W, (hd + 1) * HEAD_W) for hd in heads]
    s = [s_ref[hd] for hd in heads]
    q, k, kb, beta, lc_col, qk, t, pw = [], [], [], [], [], [], [], []
    for hd in heads:
        cq, ck = cq_all[:, sls[hd]], ck_all[:, sls[hd]]
        q.append(cq * lax.rsqrt(jnp.sum(cq * cq, axis=-1, keepdims=True) + L2_EPS) * (GD_DK ** -0.5))
        k.append(ck * lax.rsqrt(jnp.sum(ck * ck, axis=-1, keepdims=True) + L2_EPS))
        beta.append(beta_all[:, GD_HEADS + hd:GD_HEADS + hd + 1])
        lc_col.append(lc_all[:, hd:hd + 1])
        decay = jnp.exp(lc_col[hd] - lc_all_t[hd:hd + 1, :] + msk_ref[0])
        kb.append(k[hd] * beta[hd])
        p = _mm_nt(jnp.concatenate([q[hd], kb[hd]], axis=0), k[hd])
        qk.append(p[:tb] * decay)
        a = p[tb:] * decay * msk_ref[1]
        t.append(msk_ref[2] - a)
        pw.append(a)
    for _ in range(n_sq):
        for hd in heads:
            pw[hd] = _mm(pw[hd], pw[hd])
            t[hd] = t[hd] + _mm(t[hd], pw[hd])
    u, w, qe = [], [], []
    for hd in heads:
        e_lc = jnp.exp(lc_col[hd])
        uw = _mm(t[hd], jnp.concatenate([v_all[:, sls[hd]] * beta[hd], kb[hd] * e_lc], axis=1))
        u.append(uw[:, :HEAD_W])
        w.append(uw[:, HEAD_W:])
        qe.append(q[hd] * e_lc)
    outs = [[] for _ in heads]
    for c0 in range(0, tb, GD_CHUNK):
        c1 = c0 + GD_CHUNK
        for hd in heads:
            r = _mm(jnp.concatenate([w[hd][c0:c1], qe[hd][c0:c1]], axis=0), s[hd])
            v_new = u[hd][c0:c1] - r[:GD_CHUNK]
            pads = ([jnp.zeros((c0, HEAD_W), F32)] if c0 else []) + [v_new] + (
                [jnp.zeros((tb - c1, HEAD_W), F32)] if c1 < tb else [])
            outs[hd].append(r[GD_CHUNK:] + _mm(qk[hd][c0:c1, :], jnp.concatenate(pads, axis=0)))
            lc_last = lc_col[hd][c1 - 1:c1, :]
            kd = k[hd][c0:c1] * jnp.exp(lc_last - lc_col[hd][c0:c1])
            s[hd] = jnp.exp(lc_last) * s[hd] + _mm_tn(kd, v_new)
    ys = []
    for hd in heads:
        o = jnp.concatenate(outs[hd], axis=0)
        o = o * lax.rsqrt(jnp.mean(o * o, axis=-1, keepdims=True) + RMS_EPS) * nw_ref[...]
        zg = z_ref[:, sls[hd]]
        ys.append((o * (zg * _sigmoid(zg))).astype(o_ref.dtype))
    o_ref[...] = jnp.concatenate(ys, axis=1)
    for hd in heads:
        s_ref[hd] = s[hd]


def _gdn(proj, conv_w, a_log, dt_bias, norm_w, bsz, seq):
    nt = seq // TIME_BLOCK
    idx = np.arange(TIME_BLOCK)
    tri = (idx[:, None] >= idx[None, :]) & (idx[:, None] // GD_CHUNK == idx[None, :] // GD_CHUNK)
    eye = idx[:, None] == idx[None, :]
    masks = jnp.asarray(np.stack([np.where(tri, 0.0, -np.inf), 1.0 - eye, eye]).astype(np.float32))
    tri = jnp.asarray(tri.astype(np.float32), BF16)
    width = GD_HEADS * HEAD_W
    lane_pad = lambda p: jnp.concatenate([p.astype(F32), jnp.zeros((HEAD_W - GD_HEADS,), F32)]
                                         ).reshape(1, HEAD_W)
    col = lambda c0: pl.BlockSpec((TIME_BLOCK, width), lambda b, t: (b * nt + t, c0 // GD_HEADS))
    cw = lambda c0: pl.BlockSpec((CONV_K, width), lambda b, t: (0, c0))
    row = lambda: pl.BlockSpec((1, HEAD_W), lambda b, t: (0, 0))
    return pl.pallas_call(
        _gdn_kernel,
        grid=(bsz, nt),
        in_specs=[col(COL_GQ), col(COL_GK), col(COL_GV), col(COL_GZ),
                  pl.BlockSpec((TIME_BLOCK, HEAD_W), lambda b, t: (b * nt + t, COL_GAB)),
                  cw(0), cw(1), cw(2), row(), row(), row(),
                  pl.BlockSpec((TIME_BLOCK, TIME_BLOCK), lambda b, t: (0, 0)),
                  pl.BlockSpec((3, TIME_BLOCK, TIME_BLOCK), lambda b, t: (0, 0, 0))],
        out_specs=pl.BlockSpec((TIME_BLOCK, width), lambda b, t: (b * nt + t, 0)),
        out_shape=jax.ShapeDtypeStruct((bsz * seq, width), BF16),
        scratch_shapes=[pltpu.VMEM((GD_HEADS, GD_DK, HEAD_W), F32)] + [
            pltpu.VMEM((TIME_BLOCK + 8, width), F32) for _ in range(3)],
        compiler_params=_params(2),
        name="gated_deltanet",
    )(proj, proj, proj, proj, proj, conv_w, conv_w, conv_w, lane_pad(a_log), lane_pad(dt_bias),
      norm_w.reshape(1, HEAD_W), tri, masks)


def _attn_router_kernel(yh_ref, yg_ref, wh_ref, wg_ref, x0_ref, g0_ref, b0_ref, wq_ref, k_ref, v_ref,
                        wo_ref, g_ref, b_ref, wr_ref, br_ref, o_ref, route_ref):
    half = x0_ref.shape[0] // 2
    parts = [slice(0, half), slice(half, 2 * half)]
    x = []
    for rs in parts:
        mix = (jnp.dot(yh_ref[rs, :], wh_ref[...], preferred_element_type=F32)
               + jnp.dot(yg_ref[rs, :], wg_ref[...], preferred_element_type=F32))
        x.append(_layer_norm(ALPHA * x0_ref[rs, :] + mix, g0_ref[...], b0_ref[...]))
    q = [_mm(xi, wq_ref[...]).astype(BF16) for xi in x]
    heads = [[] for _ in parts]
    for h in range(MEM_HEADS):
        sl = slice(h * MEM_DH, (h + 1) * MEM_DH)
        for j in range(len(parts)):
            s = lax.dot_general(q[j][:, sl], k_ref[:, sl], NT_DIMS, preferred_element_type=F32)
            s = s * (MEM_DH ** -0.5)
            p = jnp.exp(s - jnp.max(s, axis=-1, keepdims=True))
            p = p / jnp.sum(p, axis=-1, keepdims=True)
            heads[j].append(_mm(p, v_ref[:, sl]))
    x2 = [_layer_norm(ALPHA * x[j] + _mm(jnp.concatenate(heads[j], axis=1), wo_ref[...]),
                      g_ref[...], b_ref[...]) for j in range(len(parts))]
    for j, rs in enumerate(parts):
        _store_token_tiles(o_ref.at[pl.ds(rs.start * ROW_CHUNKS, half * ROW_CHUNKS), :], x2[j])
        route_ref[rs, :] = _route_select(_mm(x2[j], wr_ref[...]) + br_ref[...])


def _route_select(logits):
    lane = lax.broadcasted_iota(jnp.int32, (1, ROUTE_W), 1).astype(F32)
    neg = -jnp.inf
    first_at = lambda val, top: jnp.min(jnp.where(val == top, lane, float(ROUTE_W)), axis=-1,
                                        keepdims=True)
    gl = jnp.where(lane < N_GROUPS, logits, neg)
    g_max = jnp.max(gl, axis=-1, keepdims=True)
    g_sel = first_at(gl, g_max)
    g_gate = 1.0 / jnp.sum(jnp.exp(gl - g_max), axis=-1, keepdims=True)
    lo = N_GROUPS + EXPERTS_PER_GROUP * g_sel
    el = jnp.where((lane >= lo) & (lane < lo + EXPERTS_PER_GROUP), logits, neg)
    t1 = jnp.max(el, axis=-1, keepdims=True)
    i1 = first_at(el, t1)
    el2 = jnp.where(lane == i1, neg, el)
    t2 = jnp.max(el2, axis=-1, keepdims=True)
    i2 = first_at(el2, t2)
    e2 = jnp.exp(t2 - t1)
    w1 = g_gate / (1.0 + e2)
    w2 = g_gate * e2 / (1.0 + e2)
    return jnp.where(lane == 0, i1 - N_GROUPS,
                     jnp.where(lane == 1, i2 - N_GROUPS,
                               jnp.where(lane == 2, w1, jnp.where(lane == 3, w2, 0.0))))


def _attn_router(y_hg, y_gd, w_out, x, kv_all, w_mq, w_mo, w_route, b_route, layer, ln_g, ln_b,
                 bsz, seq):
    n, d = x.shape
    hw = y_hg.shape[1]
    tq = ROW_TILE
    nq = seq // tq
    mlen = kv_all.shape[0] // bsz
    wspec = lambda: pl.BlockSpec((None, d, d), lambda b, i: (layer, 0, 0))
    lnspec = lambda j: pl.BlockSpec((None, 1, d), lambda b, i: (layer * 3 + j, 0, 0))
    yspec = lambda: pl.BlockSpec((tq, hw), lambda b, i: (b * nq + i, 0))
    return pl.pallas_call(
        _attn_router_kernel,
        grid=(bsz, nq),
        in_specs=[yspec(), yspec(),
                  pl.BlockSpec((None, hw, d), lambda b, i: (layer, 0, 0)),
                  pl.BlockSpec((None, hw, d), lambda b, i: (layer, 1, 0)),
                  pl.BlockSpec((tq, d), lambda b, i: (b * nq + i, 0)),
                  lnspec(0), lnspec(0),
                  wspec(),
                  pl.BlockSpec((mlen, d), lambda b, i: (b, 2 * layer)),
                  pl.BlockSpec((mlen, d), lambda b, i: (b, 2 * layer + 1)),
                  wspec(), lnspec(1), lnspec(1),
                  pl.BlockSpec((None, d, ROUTE_W), lambda b, i: (layer, 0, 0)),
                  pl.BlockSpec((None, 1, ROUTE_W), lambda b, i: (layer, 0, 0))],
        out_specs=[pl.BlockSpec((tq * ROW_CHUNKS, LANES), lambda b, i: (b * nq + i, 0)),
                   pl.BlockSpec((tq, ROUTE_W), lambda b, i: (b * nq + i, 0))],
        out_shape=[jax.ShapeDtypeStruct((n * ROW_CHUNKS, LANES), F32),
                   jax.ShapeDtypeStruct((n, ROUTE_W), F32)],
        compiler_params=_params(2),
        name="attn_router",
    )(y_hg, y_gd, w_out, w_out, x, ln_g, ln_b, w_mq, kv_all, kv_all, w_mo, ln_g, ln_b, w_route,
      b_route)


def _expert_kernel(order_ref, bstart_ref, bend_ref, be_ref, nu_ref, x_hbm, zeros_hbm, wg_ref, wu_ref,
                   wd_ref, y_hbm, rows0, rows1, yv0, yv1, gsem, ssem, zsem, wg_bf, wu_bf, wd_bf):
    b = pl.program_id(0)
    nu = nu_ref[0]
    m = order_ref.shape[0] - EXPERT_ROWS
    n_tok = m // TOP_K
    k_bits = int(math.log2(TOP_K))
    rows, yv = (rows0, rows1), (yv0, yv1)
    block_rows = EXPERT_ROWS * ROW_CHUNKS

    def tile(r):
        if isinstance(r, int):
            return pl.ds(r * ROW_CHUNKS, ROW_CHUNKS)
        return pl.ds(pl.multiple_of(r * ROW_CHUNKS, ROW_CHUNKS), ROW_CHUNKS)

    def gather_row(base, i, s):
        a = order_ref[base + i]
        pltpu.make_async_copy(x_hbm.at[tile(a >> k_bits), :], rows[s].at[tile(i), :],
                              gsem.at[s]).start()

    def scatter_row(base, n_valid, i, s):
        a = order_ref[base + i]
        row = jnp.where(i < n_valid, (a & (TOP_K - 1)) * n_tok + (a >> k_bits), m + i)
        pltpu.make_async_copy(yv[s].at[tile(i), :], y_hbm.at[tile(row), :], ssem.at[s]).start()

    def wait_gather(s):
        pltpu.make_async_copy(x_hbm.at[pl.ds(0, block_rows), :], rows[s], gsem.at[s]).wait()

    def wait_scatter(s):
        pltpu.make_async_copy(yv[s], y_hbm.at[pl.ds(0, block_rows), :], ssem.at[s]).wait()

    @pl.when(b == 0)
    def _():
        zc = pltpu.make_async_copy(zeros_hbm, y_hbm.at[pl.ds(m * ROW_CHUNKS, block_rows), :], zsem)
        zc.start()
        zc.wait()
        yv1[...] = jnp.zeros_like(yv1)
        base = bstart_ref[0]

        def body(i, carry):
            gather_row(base, i, 0)
            return carry
        lax.fori_loop(0, EXPERT_ROWS, body, 0, unroll=8)

    @pl.when(jnp.logical_or(b == 0, be_ref[b] != be_ref[jnp.maximum(b - 1, 0)]))
    def _():
        wg_bf[...] = wg_ref[...].astype(BF16)
        wu_bf[...] = wu_ref[...].astype(BF16)
        wd_bf[...] = wd_ref[...].astype(BF16)

    def step(s):
        o = 1 - s
        wait_gather(s)
        nxt_base = bstart_ref[jnp.minimum(b + 1, nu - 1)]
        for i in range(EXPERT_ROWS):
            gather_row(nxt_base, i, o)
        prev = jnp.maximum(b - 1, 0)
        prev_base = bstart_ref[prev]
        prev_valid = jnp.where(b == 0, 0, bend_ref[prev] - prev_base)
        for i in range(EXPERT_ROWS):
            scatter_row(prev_base, prev_valid, i, o)

        xb = _load_token_tiles(rows[s], EXPERT_ROWS).astype(BF16)
        hg = jnp.dot(xb, wg_bf[...], preferred_element_type=F32)
        hu = jnp.dot(xb, wu_bf[...], preferred_element_type=F32)
        hid = hg * _sigmoid(hg) * hu
        y = jnp.dot(hid.astype(BF16), wd_bf[...], preferred_element_type=F32)

        @pl.when(b >= 1)
        def _():
            wait_scatter(s)

        _store_token_tiles(yv[s], y)

        @pl.when(b == nu - 1)
        def _():
            base = bstart_ref[b]
            n_valid = bend_ref[b] - base

            def body(i, carry):
                scatter_row(base, n_valid, i, s)
                return carry
            lax.fori_loop(0, EXPERT_ROWS, body, 0, unroll=8)
            wait_gather(o)
            wait_scatter(o)
            wait_scatter(s)

    for parity in range(2):
        @pl.when(jnp.logical_and(b < nu, b % 2 == parity))
        def _():
            step(parity)


def _experts(x, order, bstart, bend, block_expert, n_used, w_gate, w_up, w_down, layer):
    d = D_MODEL
    m = order.shape[0] - EXPERT_ROWS
    block_rows = EXPERT_ROWS * ROW_CHUNKS
    nb = m // EXPERT_ROWS + N_EXPERTS
    any_spec = pl.BlockSpec(memory_space=pl.ANY)
    wspec = lambda r, c: pl.BlockSpec((None, None, r, c),
                                      lambda b, o, bs, bn, be, nu: (layer, be[b], 0, 0))
    grid_spec = pltpu.PrefetchScalarGridSpec(
        num_scalar_prefetch=5, grid=(nb,),
        in_specs=[any_spec, any_spec, wspec(d, D_EXPERT), wspec(d, D_EXPERT), wspec(D_EXPERT, d)],
        out_specs=any_spec,
        scratch_shapes=[pltpu.VMEM((block_rows, LANES), F32) for _ in range(4)] + [
                        pltpu.SemaphoreType.DMA((2,)), pltpu.SemaphoreType.DMA((2,)),
                        pltpu.SemaphoreType.DMA(()),
                        pltpu.VMEM((d, D_EXPERT), BF16), pltpu.VMEM((d, D_EXPERT), BF16),
                        pltpu.VMEM((D_EXPERT, d), BF16)])
    return pl.pallas_call(
        _expert_kernel,
        grid_spec=grid_spec,
        out_shape=jax.ShapeDtypeStruct((m * ROW_CHUNKS + block_rows, LANES), F32),
        compiler_params=_params(1),
        name="moe_experts",
    )(order, bstart, bend, block_expert, n_used, x, jnp.zeros((block_rows, LANES), F32),
      w_gate, w_up, w_down)


def _combine_ln_kernel(y0_ref, y1_ref, x_ref, route_ref, g_ref, b_ref, *rest):
    route = route_ref[...]
    tm = route.shape[0]
    ff = (route[:, 2:3] * _load_token_tiles(y0_ref, tm)
          + route[:, 3:4] * _load_token_tiles(y1_ref, tm))
    xn = _layer_norm(ALPHA * _load_token_tiles(x_ref, tm) + ff, g_ref[...], b_ref[...])
    if len(rest) == 1:
        (o_ref,) = rest
    else:
        w_ref, o_ref, proj_ref = rest
        proj_ref[...] = _mm(xn, w_ref[...])
    o_ref[...] = xn


def _combine_ln(y, x, route, layer, ln_g, ln_b, w_next=None):
    n, d = route.shape[0], D_MODEL
    tm = ROW_TILE
    lnspec = lambda: pl.BlockSpec((None, 1, d), lambda i: (layer * 3 + 2, 0, 0))
    tiles = lambda off: pl.BlockSpec((tm * ROW_CHUNKS, LANES), lambda i: (off + i, 0))
    in_specs = [tiles(0), tiles(n // tm), tiles(0), pl.BlockSpec((tm, ROUTE_W), lambda i: (i, 0)),
                lnspec(), lnspec()]
    out_specs = pl.BlockSpec((tm, d), lambda i: (i, 0))
    out_shape = jax.ShapeDtypeStruct((n, d), F32)
    args = (y, y, x, route, ln_g, ln_b)
    if w_next is not None:
        pw = w_next.shape[1]
        in_specs.append(pl.BlockSpec((d, pw), lambda i: (0, 0), pipeline_mode=pl.Buffered(1)))
        out_specs = [out_specs, pl.BlockSpec((tm, pw), lambda i: (i, 0))]
        out_shape = [out_shape, jax.ShapeDtypeStruct((n, pw), F32)]
        args += (w_next,)
    return pl.pallas_call(
        _combine_ln_kernel,
        grid=(n // tm,),
        in_specs=in_specs,
        out_specs=out_specs,
        out_shape=out_shape,
        compiler_params=_params(1),
        name="moe_combine_ln",
    )(*args)


def _routing_tables(route, n_tok):
    m = n_tok * TOP_K
    assert m * N_EXPERTS < 2 ** 31
    flat_e = route[:, 0:TOP_K].astype(jnp.int32).reshape(m)
    keys = jnp.sort(flat_e * m + jnp.arange(m, dtype=jnp.int32))
    order = jnp.concatenate([keys % m, jnp.zeros((EXPERT_ROWS,), jnp.int32)])
    bounds = jnp.searchsorted(keys, jnp.arange(N_EXPERTS + 1, dtype=jnp.int32) * m, side='left',
                              method='compare_all')
    starts = bounds[:-1].astype(jnp.int32)
    counts = (bounds[1:] - bounds[:-1]).astype(jnp.int32)
    pcounts = (counts + EXPERT_ROWS - 1) // EXPERT_ROWS * EXPERT_ROWS
    pends = jnp.cumsum(pcounts)
    n_blocks = m // EXPERT_ROWS + N_EXPERTS
    n_used = pends[-1] // EXPERT_ROWS
    blk = jnp.minimum(jnp.arange(n_blocks, dtype=jnp.int32), n_used - 1)
    block_expert = jnp.minimum(jnp.searchsorted(pends, blk * EXPERT_ROWS, side='right',
                                                method='compare_all'),
                               N_EXPERTS - 1).astype(jnp.int32)
    bstart = starts[block_expert] + blk * EXPERT_ROWS - (pends - pcounts)[block_expert]
    bend = (starts + counts)[block_expert]
    return (order, bstart.astype(jnp.int32), bend.astype(jnp.int32), block_expert,
            n_used.reshape(1).astype(jnp.int32))


def kernel(x, mem, w_in, hg_lb_logits, hg_norm_w, gd_conv_w, gd_a_log, gd_dt_bias, gd_norm_w, w_out,
           mem_ln_g, mem_ln_b, w_mq, w_mk, w_mv, w_mo, w_group, b_group, w_router, b_router,
           w_gate, w_up, w_down, ln_g, ln_b):
    bsz, seq, d = x.shape
    n = bsz * seq
    lb_all = jnp.cumsum(jax.nn.softmax(hg_lb_logits.astype(F32), axis=0), axis=0)
    lb_all = lb_all - lb_all[0]
    lb_tab = jnp.stack([jnp.log(lb_all), jnp.log1p(-lb_all), 1.0 - lb_all], axis=1)
    sp = np.cumsum((512, 512, 512, 512, 512, 512, 512, GD_HEADS, GD_HEADS, 512))
    seg = lambda a, b: w_in[:, :, (sp[a - 1] if a else 0):sp[b]]
    w_in_r = jnp.concatenate(
        [seg(0, 6), seg(9, 9), seg(7, 8), jnp.zeros((DEPTH, d, HEAD_W - 2 * GD_HEADS), F32)],
        axis=-1).astype(BF16)
    w_out_b = w_out.astype(BF16)
    w_mq_b, w_mo_b = w_mq.astype(BF16), w_mo.astype(BF16)
    w_kv = jnp.stack([w_mk, w_mv], axis=1).astype(BF16)
    w_kv = w_kv.transpose(2, 0, 1, 3).reshape(d, DEPTH * 2 * d)
    pad = ROUTE_W - N_GROUPS - N_EXPERTS
    w_route = jnp.concatenate([w_group, w_router, jnp.zeros((DEPTH, d, pad), F32)], axis=-1).astype(BF16)
    b_route = jnp.concatenate([b_group, b_router, jnp.zeros((DEPTH, pad), F32)], axis=-1)
    b_route = b_route.reshape(DEPTH, 1, ROUTE_W).astype(F32)
    ln_g3 = ln_g.reshape(DEPTH * 3, 1, d)
    ln_b3 = ln_b.reshape(DEPTH * 3, 1, d)

    mem_n = _ln(mem.reshape(-1, d), mem_ln_g, mem_ln_b, ROW_TILE)
    kv_all = _matmul(mem_n, w_kv, BF16, ROW_TILE, 2 * d)

    xt = x.reshape(n, d)
    proj = _matmul(xt, w_in_r[0], F32, ROW_TILE, PROJ_W)
    for l in range(DEPTH):
        y_hg = _hgrn(proj, lb_tab[l], hg_norm_w[l], bsz, seq)
        y_gd = _gdn(proj, gd_conv_w[l], gd_a_log[l], gd_dt_bias[l], gd_norm_w[l], bsz, seq)
        x2, route = _attn_router(y_hg, y_gd, w_out_b, xt, kv_all, w_mq_b, w_mo_b, w_route, b_route,
                                 l, ln_g3, ln_b3, bsz, seq)
        order, bstart, bend, block_expert, n_used = _routing_tables(route, n)
        y = _experts(x2, order, bstart, bend, block_expert, n_used, w_gate, w_up, w_down, l)
        if l + 1 < DEPTH:
            xt, proj = _combine_ln(y, x2, route, l, ln_g3, ln_b3, w_in_r[l + 1])
        else:
            xt = _combine_ln(y, x2, route, l, ln_g3, ln_b3)
    return xt.reshape(bsz, seq, d)
```

```python
import functools
import math

import jax
import jax.numpy as jnp
import numpy as np
from jax import lax
from jax.experimental import pallas as pl
from jax.experimental.pallas import tpu as pltpu

D_MODEL = 1024
DEPTH = 4
HG_HEADS = 4
HG_DK = 128
GD_HEADS = 4
GD_DK = 128
HEAD_W = 128
CONV_K = 4
MEM_HEADS = 4
MEM_DH = D_MODEL // MEM_HEADS
N_GROUPS = 4
EXPERTS_PER_GROUP = 8
N_EXPERTS = N_GROUPS * EXPERTS_PER_GROUP
TOP_K = 2
D_EXPERT = 512
ALPHA = (2.0 * DEPTH) ** 0.25
LN_EPS = 1e-5
RMS_EPS = 1e-6
L2_EPS = 1e-6

COL_HQ, COL_HF, COL_HI, COL_HZ, COL_GQ, COL_GK, COL_GV, COL_GZ, COL_GAB = 0, 4, 8, 12, 16, 20, 24, 28, 32
PROJ_W = 33 * HEAD_W

TIME_BLOCK = 256
GD_CHUNK = 64
ROW_TILE = 512
EXPERT_ROWS = 256
ROUTE_W = 128
LANES = 128
ROW_CHUNKS = D_MODEL // LANES
VMEM_LIMIT_BYTES = 56 * 1024 * 1024

BF16 = jnp.bfloat16
F32 = jnp.float32
NT_DIMS = (((1,), (1,)), ((), ()))
TN_DIMS = (((0,), (0,)), ((), ()))


def _params(n_grid):
    return pltpu.CompilerParams(dimension_semantics=("arbitrary",) * n_grid,
                                vmem_limit_bytes=VMEM_LIMIT_BYTES)


def _mm(a, b):
    return jnp.dot(a.astype(BF16), b.astype(BF16), preferred_element_type=F32)


def _mm_nt(a, b):
    return lax.dot_general(a.astype(BF16), b.astype(BF16), NT_DIMS, preferred_element_type=F32)


def _mm_tn(a, b):
    return lax.dot_general(a.astype(BF16), b.astype(BF16), TN_DIMS, preferred_element_type=F32)


def _cumsum_rows(tri, x):
    p0 = x.astype(BF16)
    r0 = x - p0.astype(F32)
    p1 = r0.astype(BF16)
    p2 = (r0 - p1.astype(F32)).astype(BF16)
    dot = lambda p: jnp.dot(tri, p, preferred_element_type=F32)
    return dot(p0) + dot(p1) + dot(p2)


def _sigmoid(x):
    e = jnp.exp(-jnp.abs(x))
    return jnp.where(x >= 0, 1.0, e) / (1.0 + e)


def _softplus(x):
    return jnp.maximum(x, 0.0) + jnp.log1p(jnp.exp(-jnp.abs(x)))


def _load_token_tiles(ref, n):
    return jnp.concatenate([ref[pl.ds(j, n, stride=ROW_CHUNKS), :] for j in range(ROW_CHUNKS)],
                           axis=1)


def _store_token_tiles(ref, x):
    for j in range(ROW_CHUNKS):
        ref[pl.ds(j, x.shape[0], stride=ROW_CHUNKS), :] = x[:, j * LANES:(j + 1) * LANES]


def _layer_norm(z, g, b):
    mu = jnp.mean(z, axis=-1, keepdims=True)
    zc = z - mu
    var = jnp.mean(zc * zc, axis=-1, keepdims=True)
    return zc * lax.rsqrt(var + LN_EPS) * g + b


def _matmul_kernel(x_ref, w_ref, o_ref):
    o_ref[...] = _mm(x_ref[...], w_ref[...]).astype(o_ref.dtype)


def _matmul(x, w, out_dtype, tm, tn):
    m, k = x.shape
    n = w.shape[1]
    return pl.pallas_call(
        _matmul_kernel,
        grid=(m // tm, n // tn),
        in_specs=[pl.BlockSpec((tm, k), lambda i, j: (i, 0)),
                  pl.BlockSpec((k, tn), lambda i, j: (0, j))],
        out_specs=pl.BlockSpec((tm, tn), lambda i, j: (i, j)),
        out_shape=jax.ShapeDtypeStruct((m, n), out_dtype),
        compiler_params=_params(2),
        name="matmul",
    )(x, w)


def _ln_kernel(x_ref, g_ref, b_ref, o_ref):
    o_ref[...] = _layer_norm(x_ref[...], g_ref[...], b_ref[...])


def _ln(x, g, b, tm):
    n, d = x.shape
    return pl.pallas_call(
        _ln_kernel,
        grid=(n // tm,),
        in_specs=[pl.BlockSpec((tm, d), lambda i: (i, 0)),
                  pl.BlockSpec((1, d), lambda i: (0, 0)),
                  pl.BlockSpec((1, d), lambda i: (0, 0))],
        out_specs=pl.BlockSpec((tm, d), lambda i: (i, 0)),
        out_shape=jax.ShapeDtypeStruct((n, d), F32),
        compiler_params=_params(1),
        name="mem_ln",
    )(x, g.reshape(1, d), b.reshape(1, d))


def _ref_rows(cum, row, h):
    c, w = cum.shape
    if 2 * h <= 8:
        j = row & (2 * h - 1)
        m = cum
        for dl in range(-(h - 1), h + 1):
            if dl != 0:
                m = jnp.where(j - (h - 1) == dl, pltpu.roll(cum, dl % c, axis=0), m)
        return m
    pieces = [jnp.broadcast_to(cum[s + h - 1:s + h, :], (2 * h, w)) for s in range(0, c, 2 * h)]
    return jnp.concatenate(pieces, axis=0) if len(pieces) > 1 else pieces[0]


def _hgrn_kernel(q_ref, f_ref, i_ref, z_ref, lb_ref, nw_ref, tri_ref, msk_ref, o_ref, st_ref):
    @pl.when(pl.program_id(1) == 0)
    def _():
        st_ref[...] = jnp.zeros_like(st_ref)

    c = q_ref.shape[0]
    zf = f_ref[...]
    log_lb, log1m_lb, one_m_lb = lb_ref[0:1, :], lb_ref[1:2, :], lb_ref[2:3, :]
    e = jnp.exp(-jnp.abs(zf))
    log_sig = jnp.minimum(zf, 0.0) - jnp.log1p(e)
    b = log1m_lb + log_sig
    log_f = jnp.maximum(log_lb, b) + jnp.log1p(jnp.exp(-jnp.abs(log_lb - b)))
    k_all = one_m_lb * (jnp.where(zf >= 0, e, 1.0) / (1.0 + e))
    cum_all = _cumsum_rows(tri_ref[...], log_f)

    row = lax.broadcasted_iota(jnp.int32, (c, 1), 0)
    n_lvl = msk_ref.shape[0]
    states = [st_ref[hd] for hd in range(HG_HEADS)]
    ys = []
    for hd in range(HG_HEADS):
        sl = slice(hd * HEAD_W, (hd + 1) * HEAD_W)
        q, k, v, cum = q_ref[:, sl], k_all[:, sl], i_ref[:, sl], cum_all[:, sl]
        scores = msk_ref[n_lvl - 1] * jnp.sum(q * k, axis=-1, keepdims=True)
        for lvl in range(n_lvl):
            h = 1 << lvl
            x = jnp.exp(-jnp.abs(cum - _ref_rows(cum, row, h)))
            upper = (row & (2 * h - 1)) >= h
            sc = _mm_nt(jnp.where(upper, q * x, 0.0), jnp.where(upper, 0.0, k * x))
            scores = scores + (sc * msk_ref[lvl] if 2 * h < c else sc)

        st = states[hd]
        o = _mm_nt(q * jnp.exp(cum), st) + _mm(scores, v)
        cum_last = cum[c - 1:c, :]
        states[hd] = jnp.exp(cum_last) * st + _mm_tn(v, k * jnp.exp(cum_last - cum))
        o = o * lax.rsqrt(jnp.mean(o * o, axis=-1, keepdims=True) + RMS_EPS) * nw_ref[...]
        ys.append((o * _sigmoid(z_ref[:, sl])).astype(o_ref.dtype))
    o_ref[...] = jnp.concatenate(ys, axis=1)
    for hd in range(HG_HEADS):
        st_ref[hd] = states[hd]


def _hgrn(proj, lb_tab, norm_w, bsz, seq):
    nt = seq // TIME_BLOCK
    width = HG_HEADS * HEAD_W
    idx = np.arange(TIME_BLOCK)
    tri = jnp.asarray((idx[:, None] >= idx[None, :]).astype(np.float32), BF16)
    n_lvl = int(math.log2(TIME_BLOCK))
    masks = [(idx[:, None] >> (l + 1)) == (idx[None, :] >> (l + 1)) for l in range(n_lvl - 1)]
    masks = jnp.asarray(np.stack(masks + [idx[:, None] == idx[None, :]]).astype(np.float32))
    col = lambda c0: pl.BlockSpec((TIME_BLOCK, width), lambda b, t: (b * nt + t, c0 // HG_HEADS))
    return pl.pallas_call(
        _hgrn_kernel,
        grid=(bsz, nt),
        in_specs=[col(COL_HQ), col(COL_HF), col(COL_HI), col(COL_HZ),
                  pl.BlockSpec((3, width), lambda b, t: (0, 0)),
                  pl.BlockSpec((1, HEAD_W), lambda b, t: (0, 0)),
                  pl.BlockSpec((TIME_BLOCK, TIME_BLOCK), lambda b, t: (0, 0)),
                  pl.BlockSpec((n_lvl, TIME_BLOCK, TIME_BLOCK), lambda b, t: (0, 0, 0))],
        out_specs=pl.BlockSpec((TIME_BLOCK, width), lambda b, t: (b * nt + t, 0)),
        out_shape=jax.ShapeDtypeStruct((bsz * seq, width), BF16),
        scratch_shapes=[pltpu.VMEM((HG_HEADS, HEAD_W, HG_DK), F32)],
        compiler_params=_params(2),
        name="hgrn2",
    )(proj, proj, proj, proj, lb_tab, norm_w.reshape(1, HEAD_W), tri, masks)


def _conv_silu(x_ref, w_ref, buf_ref, first):
    tb = x_ref.shape[0]

    @pl.when(first)
    def _():
        buf_ref[0:8, :] = jnp.zeros((8, buf_ref.shape[1]), F32)

    @pl.when(jnp.logical_not(first))
    def _():
        buf_ref[0:8, :] = buf_ref[tb:tb + 8, :]

    buf_ref[8:tb + 8, :] = x_ref[...]
    y = w_ref[CONV_K - 1:CONV_K, :] * x_ref[...]
    for j in range(1, CONV_K):
        y = y + w_ref[CONV_K - 1 - j:CONV_K - j, :] * buf_ref[8 - j:8 - j + tb, :]
    return y * _sigmoid(y)


def _gdn_kernel(q_ref, k_ref, v_ref, z_ref, ab_ref, wq_ref, wk_ref, wv_ref, alog_ref, dtb_ref,
                nw_ref, tri_ref, msk_ref, o_ref, s_ref, bq_ref, bk_ref, bv_ref):
    first = pl.program_id(1) == 0

    @pl.when(first)
    def _():
        s_ref[...] = jnp.zeros_like(s_ref)

    tb = q_ref.shape[0]
    cq_all = _conv_silu(q_ref, wq_ref, bq_ref, first)
    ck_all = _conv_silu(k_ref, wk_ref, bk_ref, first)
    v_all = _conv_silu(v_ref, wv_ref, bv_ref, first)

    lane = lax.broadcasted_iota(jnp.int32, (1, HEAD_W), 1)
    ab = ab_ref[...]
    g_all = jnp.where(lane < GD_HEADS, -jnp.exp(alog_ref[...]) * _softplus(ab + dtb_ref[...]), 0.0)
    beta_all = _sigmoid(ab)
    lc_all = _cumsum_rows(tri_ref[...], g_all)
    lc_all_t = jnp.transpose(lc_all)
    n_sq = int(math.log2(GD_CHUNK)) - 1

    heads = range(GD_HEADS)
    sls = [slice(hd * HEAD_W, (hd + 1) * HEAD_W) for hd in heads]
    s = [s_ref[hd] for hd in heads]
    q, k, kb, beta, lc_col, qk, t, pw = [], [], [], [], [], [], [], []
    for hd in heads:
        cq, ck = cq_all[:, sls[hd]], ck_all[:, sls[hd]]
        q.append(cq * lax.rsqrt(jnp.sum(cq * cq, axis=-1, keepdims=True) + L2_EPS) * (GD_DK ** -0.5))
        k.append(ck * lax.rsqrt(jnp.sum(ck * ck, axis=-1, keepdims=True) + L2_EPS))
        beta.append(beta_all[:, GD_HEADS + hd:GD_HEADS + hd + 1])
        lc_col.append(lc_all[:, hd:hd + 1])
        decay = jnp.exp(lc_col[hd] - lc_all_t[hd:hd + 1, :] + msk_ref[0])
        kb.append(k[hd] * beta[hd])
        p = _mm_nt(jnp.concatenate([q[hd], kb[hd]], axis=0), k[hd])
        qk.append(p[:tb] * decay)
        a = p[tb:] * decay * msk_ref[1]
        t.append(msk_ref[2] - a)
        pw.append(a)
    for _ in range(n_sq):
        for hd in heads:
            pw[hd] = _mm(pw[hd], pw[hd])
            t[hd] = t[hd] + _mm(t[hd], pw[hd])
    u, w, qe = [], [], []
    for hd in heads:
        e_lc = jnp.exp(lc_col[hd])
        uw = _mm(t[hd], jnp.concatenate([v_all[:, sls[hd]] * beta[hd], kb[hd] * e_lc], axis=1))
        u.append(uw[:, :HEAD_W])
        w.append(uw[:, HEAD_W:])
        qe.append(q[hd] * e_lc)
    outs = [[] for _ in heads]
    for c0 in range(0, tb, GD_CHUNK):
        c1 = c0 + GD_CHUNK
        for hd in heads:
            r = _mm(jnp.concatenate([w[hd][c0:c1], qe[hd][c0:c1]], axis=0), s[hd])
            v_new = u[hd][c0:c1] - r[:GD_CHUNK]
            pads = ([jnp.zeros((c0, HEAD_W), F32)] if c0 else []) + [v_new] + (
                [jnp.zeros((tb - c1, HEAD_W), F32)] if c1 < tb else [])
            outs[hd].append(r[GD_CHUNK:] + _mm(qk[hd][c0:c1, :], jnp.concatenate(pads, axis=0)))
            lc_last = lc_col[hd][c1 - 1:c1, :]
            kd = k[hd][c0:c1] * jnp.exp(lc_last - lc_col[hd][c0:c1])
            s[hd] = jnp.exp(lc_last) * s[hd] + _mm_tn(kd, v_new)
    ys = []
    for hd in heads:
        o = jnp.concatenate(outs[hd], axis=0)
        o = o * lax.rsqrt(jnp.mean(o * o, axis=-1, keepdims=True) + RMS_EPS) * nw_ref[...]
        zg = z_ref[:, sls[hd]]
        ys.append((o * (zg * _sigmoid(zg))).astype(o_ref.dtype))
    o_ref[...] = jnp.concatenate(ys, axis=1)
    for hd in heads:
        s_ref[hd] = s[hd]


def _gdn(proj, conv_w, a_log, dt_bias, norm_w, bsz, seq):
    nt = seq // TIME_BLOCK
    idx = np.arange(TIME_BLOCK)
    tri = (idx[:, None] >= idx[None, :]) & (idx[:, None] // GD_CHUNK == idx[None, :] // GD_CHUNK)
    eye = idx[:, None] == idx[None, :]
    masks = jnp.asarray(np.stack([np.where(tri, 0.0, -np.inf), 1.0 - eye, eye]).astype(np.float32))
    tri = jnp.asarray(tri.astype(np.float32), BF16)
    width = GD_HEADS * HEAD_W
    lane_pad = lambda p: jnp.concatenate([p.astype(F32), jnp.zeros((HEAD_W - GD_HEADS,), F32)]
                                         ).reshape(1, HEAD_W)
    col = lambda c0: pl.BlockSpec((TIME_BLOCK, width), lambda b, t: (b * nt + t, c0 // GD_HEADS))
    cw = lambda c0: pl.BlockSpec((CONV_K, width), lambda b, t: (0, c0))
    row = lambda: pl.BlockSpec((1, HEAD_W), lambda b, t: (0, 0))
    return pl.pallas_call(
        _gdn_kernel,
        grid=(bsz, nt),
        in_specs=[col(COL_GQ), col(COL_GK), col(COL_GV), col(COL_GZ),
                  pl.BlockSpec((TIME_BLOCK, HEAD_W), lambda b, t: (b * nt + t, COL_GAB)),
                  cw(0), cw(1), cw(2), row(), row(), row(),
                  pl.BlockSpec((TIME_BLOCK, TIME_BLOCK), lambda b, t: (0, 0)),
                  pl.BlockSpec((3, TIME_BLOCK, TIME_BLOCK), lambda b, t: (0, 0, 0))],
        out_specs=pl.BlockSpec((TIME_BLOCK, width), lambda b, t: (b * nt + t, 0)),
        out_shape=jax.ShapeDtypeStruct((bsz * seq, width), BF16),
        scratch_shapes=[pltpu.VMEM((GD_HEADS, GD_DK, HEAD_W), F32)] + [
            pltpu.VMEM((TIME_BLOCK + 8, width), F32) for _ in range(3)],
        compiler_params=_params(2),
        name="gated_deltanet",
    )(proj, proj, proj, proj, proj, conv_w, conv_w, conv_w, lane_pad(a_log), lane_pad(dt_bias),
      norm_w.reshape(1, HEAD_W), tri, masks)


def _attn_router_kernel(yh_ref, yg_ref, wh_ref, wg_ref, x0_ref, g0_ref, b0_ref, wq_ref, k_ref, v_ref,
                        wo_ref, g_ref, b_ref, wr_ref, br_ref, o_ref, route_ref):
    half = x0_ref.shape[0] // 2
    parts = [slice(0, half), slice(half, 2 * half)]
    x = []
    for rs in parts:
        mix = (jnp.dot(yh_ref[rs, :], wh_ref[...], preferred_element_type=F32)
               + jnp.dot(yg_ref[rs, :], wg_ref[...], preferred_element_type=F32))
        x.append(_layer_norm(ALPHA * x0_ref[rs, :] + mix, g0_ref[...], b0_ref[...]))
    q = [_mm(xi, wq_ref[...]).astype(BF16) for xi in x]
    heads = [[] for _ in parts]
    for h in range(MEM_HEADS):
        sl = slice(h * MEM_DH, (h + 1) * MEM_DH)
        for j in range(len(parts)):
            s = lax.dot_general(q[j][:, sl], k_ref[:, sl], NT_DIMS, preferred_element_type=F32)
            s = s * (MEM_DH ** -0.5)
            p = jnp.exp(s - jnp.max(s, axis=-1, keepdims=True))
            p = p / jnp.sum(p, axis=-1, keepdims=True)
            heads[j].append(_mm(p, v_ref[:, sl]))
    x2 = [_layer_norm(ALPHA * x[j] + _mm(jnp.concatenate(heads[j], axis=1), wo_ref[...]),
                      g_ref[...], b_ref[...]) for j in range(len(parts))]
    for j, rs in enumerate(parts):
        _store_token_tiles(o_ref.at[pl.ds(rs.start * ROW_CHUNKS, half * ROW_CHUNKS), :], x2[j])
        route_ref[rs, :] = _route_select(_mm(x2[j], wr_ref[...]) + br_ref[...])


def _route_select(logits):
    lane = lax.broadcasted_iota(jnp.int32, (1, ROUTE_W), 1).astype(F32)
    neg = -jnp.inf
    first_at = lambda val, top: jnp.min(jnp.where(val == top, lane, float(ROUTE_W)), axis=-1,
                                        keepdims=True)
    gl = jnp.where(lane < N_GROUPS, logits, neg)
    g_max = jnp.max(gl, axis=-1, keepdims=True)
    g_sel = first_at(gl, g_max)
    g_gate = 1.0 / jnp.sum(jnp.exp(gl - g_max), axis=-1, keepdims=True)
    lo = N_GROUPS + EXPERTS_PER_GROUP * g_sel
    el = jnp.where((lane >= lo) & (lane < lo + EXPERTS_PER_GROUP), logits, neg)
    t1 = jnp.max(el, axis=-1, keepdims=True)
    i1 = first_at(el, t1)
    el2 = jnp.where(lane == i1, neg, el)
    t2 = jnp.max(el2, axis=-1, keepdims=True)
    i2 = first_at(el2, t2)
    e2 = jnp.exp(t2 - t1)
    w1 = g_gate / (1.0 + e2)
    w2 = g_gate * e2 / (1.0 + e2)
    return jnp.where(lane == 0, i1 - N_GROUPS,
                     jnp.where(lane == 1, i2 - N_GROUPS,
                               jnp.where(lane == 2, w1, jnp.where(lane == 3, w2, 0.0))))


def _attn_router(y_hg, y_gd, w_out, x, kv_all, w_mq, w_mo, w_route, b_route, layer, ln_g, ln_b,
                 bsz, seq):
    n, d = x.shape
    hw = y_hg.shape[1]
    tq = ROW_TILE
    nq = seq // tq
    mlen = kv_all.shape[0] // bsz
    wspec = lambda: pl.BlockSpec((None, d, d), lambda b, i: (layer, 0, 0))
    lnspec = lambda j: pl.BlockSpec((None, 1, d), lambda b, i: (layer * 3 + j, 0, 0))
    yspec = lambda: pl.BlockSpec((tq, hw), lambda b, i: (b * nq + i, 0))
    return pl.pallas_call(
        _attn_router_kernel,
        grid=(bsz, nq),
        in_specs=[yspec(), yspec(),
                  pl.BlockSpec((None, hw, d), lambda b, i: (layer, 0, 0)),
                  pl.BlockSpec((None, hw, d), lambda b, i: (layer, 1, 0)),
                  pl.BlockSpec((tq, d), lambda b, i: (b * nq + i, 0)),
                  lnspec(0), lnspec(0),
                  wspec(),
                  pl.BlockSpec((mlen, d), lambda b, i: (b, 2 * layer)),
                  pl.BlockSpec((mlen, d), lambda b, i: (b, 2 * layer + 1)),
                  wspec(), lnspec(1), lnspec(1),
                  pl.BlockSpec((None, d, ROUTE_W), lambda b, i: (layer, 0, 0)),
                  pl.BlockSpec((None, 1, ROUTE_W), lambda b, i: (layer, 0, 0))],
        out_specs=[pl.BlockSpec((tq * ROW_CHUNKS, LANES), lambda b, i: (b * nq + i, 0)),
                   pl.BlockSpec((tq, ROUTE_W), lambda b, i: (b * nq + i, 0))],
        out_shape=[jax.ShapeDtypeStruct((n * ROW_CHUNKS, LANES), F32),
                   jax.ShapeDtypeStruct((n, ROUTE_W), F32)],
        compiler_params=_params(2),
        name="attn_router",
    )(y_hg, y_gd, w_out, w_out, x, ln_g, ln_b, w_mq, kv_all, kv_all, w_mo, ln_g, ln_b, w_route,
      b_route)


def _expert_kernel(order_ref, bstart_ref, bend_ref, be_ref, nu_ref, x_hbm, zeros_hbm, wg_ref, wu_ref,
                   wd_ref, y_hbm, rows0, rows1, yv0, yv1, gsem, ssem, zsem, wg_bf, wu_bf, wd_bf):
    b = pl.program_id(0)
    nu = nu_ref[0]
    m = order_ref.shape[0] - EXPERT_ROWS
    n_tok = m // TOP_K
    k_bits = int(math.log2(TOP_K))
    rows, yv = (rows0, rows1), (yv0, yv1)
    block_rows = EXPERT_ROWS * ROW_CHUNKS

    def tile(r):
        if isinstance(r, int):
            return pl.ds(r * ROW_CHUNKS, ROW_CHUNKS)
        return pl.ds(pl.multiple_of(r * ROW_CHUNKS, ROW_CHUNKS), ROW_CHUNKS)

    def priority(i):
        return i % 2 if isinstance(i, int) else 0

    def gather_row(base, i, s):
        a = order_ref[base + i]
        pltpu.make_async_copy(x_hbm.at[tile(a >> k_bits), :], rows[s].at[tile(i), :],
                              gsem.at[s]).start(priority=priority(i))

    def scatter_row(base, n_valid, i, s):
        a = order_ref[base + i]
        row = jnp.where(i < n_valid, (a & (TOP_K - 1)) * n_tok + (a >> k_bits), m + i)
        pltpu.make_async_copy(yv[s].at[tile(i), :], y_hbm.at[tile(row), :],
                              ssem.at[s]).start(priority=priority(i))

    def wait_gather(s):
        pltpu.make_async_copy(x_hbm.at[pl.ds(0, block_rows), :], rows[s], gsem.at[s]).wait()

    def wait_scatter(s):
        pltpu.make_async_copy(yv[s], y_hbm.at[pl.ds(0, block_rows), :], ssem.at[s]).wait()

    @pl.when(b == 0)
    def _():
        zc = pltpu.make_async_copy(zeros_hbm, y_hbm.at[pl.ds(m * ROW_CHUNKS, block_rows), :], zsem)
        zc.start()
        zc.wait()
        yv1[...] = jnp.zeros_like(yv1)
        base = bstart_ref[0]

        def body(i, carry):
            gather_row(base, i, 0)
            return carry
        lax.fori_loop(0, EXPERT_ROWS, body, 0, unroll=8)

    @pl.when(jnp.logical_or(b == 0, be_ref[b] != be_ref[jnp.maximum(b - 1, 0)]))
    def _():
        wg_bf[...] = wg_ref[...].astype(BF16)
        wu_bf[...] = wu_ref[...].astype(BF16)
        wd_bf[...] = wd_ref[...].astype(BF16)

    def step(s):
        o = 1 - s
        wait_gather(s)
        nxt_base = bstart_ref[jnp.minimum(b + 1, nu - 1)]
        for i in range(EXPERT_ROWS):
            gather_row(nxt_base, i, o)
        prev = jnp.maximum(b - 1, 0)
        prev_base = bstart_ref[prev]
        prev_valid = jnp.where(b == 0, 0, bend_ref[prev] - prev_base)
        for i in range(EXPERT_ROWS):
            scatter_row(prev_base, prev_valid, i, o)

        xb = _load_token_tiles(rows[s], EXPERT_ROWS).astype(BF16)
        hg = jnp.dot(xb, wg_bf[...], preferred_element_type=F32)
        hu = jnp.dot(xb, wu_bf[...], preferred_element_type=F32)
        hid = hg * _sigmoid(hg) * hu
        y = jnp.dot(hid.astype(BF16), wd_bf[...], preferred_element_type=F32)

        @pl.when(b >= 1)
        def _():
            wait_scatter(s)

        _store_token_tiles(yv[s], y)

        @pl.when(b == nu - 1)
        def _():
            base = bstart_ref[b]
            n_valid = bend_ref[b] - base

            def body(i, carry):
                scatter_row(base, n_valid, i, s)
                return carry
            lax.fori_loop(0, EXPERT_ROWS, body, 0, unroll=8)
            wait_gather(o)
            wait_scatter(o)
            wait_scatter(s)

    for parity in range(2):
        @pl.when(jnp.logical_and(b < nu, b % 2 == parity))
        def _():
            step(parity)


def _experts(x, order, bstart, bend, block_expert, n_used, w_gate, w_up, w_down, layer):
    d = D_MODEL
    m = order.shape[0] - EXPERT_ROWS
    block_rows = EXPERT_ROWS * ROW_CHUNKS
    nb = m // EXPERT_ROWS + N_EXPERTS
    any_spec = pl.BlockSpec(memory_space=pl.ANY)
    wspec = lambda r, c: pl.BlockSpec((None, None, r, c),
                                      lambda b, o, bs, bn, be, nu: (layer, be[b], 0, 0))
    grid_spec = pltpu.PrefetchScalarGridSpec(
        num_scalar_prefetch=5, grid=(nb,),
        in_specs=[any_spec, any_spec, wspec(d, D_EXPERT), wspec(d, D_EXPERT), wspec(D_EXPERT, d)],
        out_specs=any_spec,
        scratch_shapes=[pltpu.VMEM((block_rows, LANES), F32) for _ in range(4)] + [
                        pltpu.SemaphoreType.DMA((2,)), pltpu.SemaphoreType.DMA((2,)),
                        pltpu.SemaphoreType.DMA(()),
                        pltpu.VMEM((d, D_EXPERT), BF16), pltpu.VMEM((d, D_EXPERT), BF16),
                        pltpu.VMEM((D_EXPERT, d), BF16)])
    return pl.pallas_call(
        _expert_kernel,
        grid_spec=grid_spec,
        out_shape=jax.ShapeDtypeStruct((m * ROW_CHUNKS + block_rows, LANES), F32),
        compiler_params=_params(1),
        name="moe_experts",
    )(order, bstart, bend, block_expert, n_used, x, jnp.zeros((block_rows, LANES), F32),
      w_gate, w_up, w_down)


def _combine_ln_kernel(y0_ref, y1_ref, x_ref, route_ref, g_ref, b_ref, *rest):
    route = route_ref[...]
    tm = route.shape[0]
    ff = (route[:, 2:3] * _load_token_tiles(y0_ref, tm)
          + route[:, 3:4] * _load_token_tiles(y1_ref, tm))
    xn = _layer_norm(ALPHA * _load_token_tiles(x_ref, tm) + ff, g_ref[...], b_ref[...])
    if len(rest) == 1:
        (o_ref,) = rest
    else:
        w_ref, o_ref, proj_ref = rest
        proj_ref[...] = _mm(xn, w_ref[...])
    o_ref[...] = xn


def _combine_ln(y, x, route, layer, ln_g, ln_b, w_next=None):
    n, d = route.shape[0], D_MODEL
    tm = ROW_TILE
    lnspec = lambda: pl.BlockSpec((None, 1, d), lambda i: (layer * 3 + 2, 0, 0))
    tiles = lambda off: pl.BlockSpec((tm * ROW_CHUNKS, LANES), lambda i: (off + i, 0))
    in_specs = [tiles(0), tiles(n // tm), tiles(0), pl.BlockSpec((tm, ROUTE_W), lambda i: (i, 0)),
                lnspec(), lnspec()]
    out_specs = pl.BlockSpec((tm, d), lambda i: (i, 0))
    out_shape = jax.ShapeDtypeStruct((n, d), F32)
    args = (y, y, x, route, ln_g, ln_b)
    if w_next is not None:
        pw = w_next.shape[1]
        in_specs.append(pl.BlockSpec((d, pw), lambda i: (0, 0), pipeline_mode=pl.Buffered(1)))
        out_specs = [out_specs, pl.BlockSpec((tm, pw), lambda i: (i, 0))]
        out_shape = [out_shape, jax.ShapeDtypeStruct((n, pw), F32)]
        args += (w_next,)
    return pl.pallas_call(
        _combine_ln_kernel,
        grid=(n // tm,),
        in_specs=in_specs,
        out_specs=out_specs,
        out_shape=out_shape,
        compiler_params=_params(1),
        name="moe_combine_ln",
    )(*args)


def _routing_tables(route, n_tok):
    m = n_tok * TOP_K
    assert m * N_EXPERTS < 2 ** 31
    flat_e = route[:, 0:TOP_K].astype(jnp.int32).reshape(m)
    keys = jnp.sort(flat_e * m + jnp.arange(m, dtype=jnp.int32))
    order = jnp.concatenate([keys % m, jnp.zeros((EXPERT_ROWS,), jnp.int32)])
    bounds = jnp.searchsorted(keys, jnp.arange(N_EXPERTS + 1, dtype=jnp.int32) * m, side='left',
                              method='compare_all')
    starts = bounds[:-1].astype(jnp.int32)
    counts = (bounds[1:] - bounds[:-1]).astype(jnp.int32)
    pcounts = (counts + EXPERT_ROWS - 1) // EXPERT_ROWS * EXPERT_ROWS
    pends = jnp.cumsum(pcounts)
    n_blocks = m // EXPERT_ROWS + N_EXPERTS
    n_used = pends[-1] // EXPERT_ROWS
    blk = jnp.minimum(jnp.arange(n_blocks, dtype=jnp.int32), n_used - 1)
    block_expert = jnp.minimum(jnp.searchsorted(pends, blk * EXPERT_ROWS, side='right',
                                                method='compare_all'),
                               N_EXPERTS - 1).astype(jnp.int32)
    bstart = starts[block_expert] + blk * EXPERT_ROWS - (pends - pcounts)[block_expert]
    bend = (starts + counts)[block_expert]
    return (order, bstart.astype(jnp.int32), bend.astype(jnp.int32), block_expert,
            n_used.reshape(1).astype(jnp.int32))


def kernel(x, mem, w_in, hg_lb_logits, hg_norm_w, gd_conv_w, gd_a_log, gd_dt_bias, gd_norm_w, w_out,
           mem_ln_g, mem_ln_b, w_mq, w_mk, w_mv, w_mo, w_group, b_group, w_router, b_router,
           w_gate, w_up, w_down, ln_g, ln_b):
    bsz, seq, d = x.shape
    n = bsz * seq
    lb_all = jnp.cumsum(jax.nn.softmax(hg_lb_logits.astype(F32), axis=0), axis=0)
    lb_all = lb_all - lb_all[0]
    lb_tab = jnp.stack([jnp.log(lb_all), jnp.log1p(-lb_all), 1.0 - lb_all], axis=1)
    sp = np.cumsum((512, 512, 512, 512, 512, 512, 512, GD_HEADS, GD_HEADS, 512))
    seg = lambda a, b: w_in[:, :, (sp[a - 1] if a else 0):sp[b]]
    w_in_r = jnp.concatenate(
        [seg(0, 6), seg(9, 9), seg(7, 8), jnp.zeros((DEPTH, d, HEAD_W - 2 * GD_HEADS), F32)],
        axis=-1).astype(BF16)
    w_out_b = w_out.astype(BF16)
    w_mq_b, w_mo_b = w_mq.astype(BF16), w_mo.astype(BF16)
    w_kv = jnp.stack([w_mk, w_mv], axis=1).astype(BF16)
    w_kv = w_kv.transpose(2, 0, 1, 3).reshape(d, DEPTH * 2 * d)
    pad = ROUTE_W - N_GROUPS - N_EXPERTS
    w_route = jnp.concatenate([w_group, w_router, jnp.zeros((DEPTH, d, pad), F32)], axis=-1).astype(BF16)
    b_route = jnp.concatenate([b_group, b_router, jnp.zeros((DEPTH, pad), F32)], axis=-1)
    b_route = b_route.reshape(DEPTH, 1, ROUTE_W).astype(F32)
    ln_g3 = ln_g.reshape(DEPTH * 3, 1, d)
    ln_b3 = ln_b.reshape(DEPTH * 3, 1, d)

    mem_n = _ln(mem.reshape(-1, d), mem_ln_g, mem_ln_b, ROW_TILE)
    kv_all = _matmul(mem_n, w_kv, BF16, ROW_TILE, 2 * d)

    xt = x.reshape(n, d)
    proj = _matmul(xt, w_in_r[0], F32, ROW_TILE, PROJ_W)
    for l in range(DEPTH):
        y_hg = _hgrn(proj, lb_tab[l], hg_norm_w[l], bsz, seq)
        y_gd = _gdn(proj, gd_conv_w[l], gd_a_log[l], gd_dt_bias[l], gd_norm_w[l], bsz, seq)
        x2, route = _attn_router(y_hg, y_gd, w_out_b, xt, kv_all, w_mq_b, w_mo_b, w_route, b_route,
                                 l, ln_g3, ln_b3, bsz, seq)
        order, bstart, bend, block_expert, n_used = _routing_tables(route, n)
        y = _experts(x2, order, bstart, bend, block_expert, n_used, w_gate, w_up, w_down, l)
        if l + 1 < DEPTH:
            xt, proj = _combine_ln(y, x2, route, l, ln_g3, ln_b3, w_in_r[l + 1])
        else:
            xt = _combine_ln(y, x2, route, l, ln_g3, ln_b3)
    return xt.reshape(bsz, seq, d)
```
